```python
import math
import jax
import jax.numpy as jnp
from jax import lax
import numpy as np

D_MODEL = 1024
BATCH = 16
SEQ = 256
DEPTH = 4
DEC_BATCH = 8
DEC_SEQ = 4096
PAST_LEN = 512

GRID_W = 64
HEAD_DIM = 64
N_Q_HEADS = 8
N_KV_HEADS = 2
Q_PER_KV = N_Q_HEADS // N_KV_HEADS
ATTN_WIDTH = N_Q_HEADS * HEAD_DIM
KV_WIDTH = N_KV_HEADS * HEAD_DIM
POOL_WINDOWS = (2, 4, 8, 16)
N_POOL_GROUPS = len(POOL_WINDOWS)
POOL_WIDTH = D_MODEL - ATTN_WIDTH
POOL_GROUP_WIDTH = POOL_WIDTH // N_POOL_GROUPS
MIX_WIDTH = ATTN_WIDTH + POOL_WIDTH
IN_PROJ_WIDTH = ATTN_WIDTH + 2 * KV_WIDTH + POOL_WIDTH
N_EXPERTS = 16
N_EXPERT_GROUPS = 4
EXPERTS_PER_GROUP = N_EXPERTS // N_EXPERT_GROUPS
TOP_K = 2
D_EXPERT = 512
Q_BLOCK = 128
ROPE_THETA = 10000.0
EPS = 1e-6

kernel_name = "hybrid_pool_gqa_groupedmoe_diffusion_step"


def rms_norm(x, g):
    xf = x.astype(jnp.float32)
    y = xf * lax.rsqrt(jnp.mean(xf * xf, axis=-1, keepdims=True) + EPS)
    return (y * g.astype(jnp.float32)).astype(x.dtype)


def modulation(cond_act, w, b):
    m = cond_act @ w + b
    return tuple(t[:, None, :] for t in jnp.split(m, 6, axis=-1))


def modulate(h, shift, scale):
    return h * (1 + scale) + shift


def in_projection(h, w_in):
    z = h @ w_in
    B, T, _ = z.shape
    q = z[..., :ATTN_WIDTH].reshape(B, T, N_Q_HEADS, HEAD_DIM)
    k = z[..., ATTN_WIDTH:ATTN_WIDTH + KV_WIDTH].reshape(B, T, N_KV_HEADS, HEAD_DIM)
    v = z[..., ATTN_WIDTH + KV_WIDTH:ATTN_WIDTH + 2 * KV_WIDTH].reshape(B, T, N_KV_HEADS, HEAD_DIM)
    p = z[..., ATTN_WIDTH + 2 * KV_WIDTH:]
    return q, k, v, p


def axial_angles(row_idx, col_idx):
    half = HEAD_DIM // 2
    inv_freq = ROPE_THETA ** (-jnp.arange(0, half, 2, dtype=jnp.float32) / half)
    ang_r = row_idx.astype(jnp.float32)[:, None] * inv_freq[None, :]
    ang_c = col_idx.astype(jnp.float32)[:, None] * inv_freq[None, :]
    return ang_r, ang_c


def rotate_axis(x, ang):
    n = x.shape[-1] // 2
    x1, x2 = x[..., :n], x[..., n:]
    cos = jnp.cos(ang)[None, :, None, :]
    sin = jnp.sin(ang)[None, :, None, :]
    return jnp.concatenate([x1 * cos - x2 * sin, x1 * sin + x2 * cos], axis=-1)


def apply_axial_rope(x, ang_r, ang_c):
    half = HEAD_DIM // 2
    xf = x.astype(jnp.float32)
    y = jnp.concatenate([rotate_axis(xf[..., :half], ang_r),
                         rotate_axis(xf[..., half:], ang_c)], axis=-1)
    return y.astype(x.dtype)


def blocked_gqa_attention(q, k, v):
    B, T, H, Dh = q.shape
    n_blk = T // Q_BLOCK
    qb = q.reshape(B, n_blk, Q_BLOCK, N_KV_HEADS, Q_PER_KV, Dh).transpose(1, 0, 2, 3, 4, 5)
    kf = k.astype(jnp.float32)
    vf = v.astype(jnp.float32)
    scale = Dh ** -0.5

    def one_block(q_blk):
        s = jnp.einsum("bqkgd,bskd->bkgqs", q_blk.astype(jnp.float32), kf) * scale
        p = jax.nn.softmax(s, axis=-1)
        return jnp.einsum("bkgqs,bskd->bqkgd", p, vf).astype(q.dtype)

    o = lax.map(one_block, qb)
    return o.transpose(1, 0, 2, 3, 4, 5).reshape(B, T, H * Dh)


def multiscale_pool(p, w_pool, scale):
    B, T, _ = p.shape
    pf = p.astype(jnp.float32).reshape(B, T, N_POOL_GROUPS, POOL_GROUP_WIDTH)
    cs = jnp.concatenate([jnp.zeros((B, 1, N_POOL_GROUPS, POOL_GROUP_WIDTH), jnp.float32),
                          jnp.cumsum(pf, axis=1)], axis=1)
    t = jnp.arange(T, dtype=jnp.int32)
    outs = []
    for g, w in enumerate(POOL_WINDOWS):
        lo = jnp.clip(t - w // 2, 0, T)
        hi = jnp.clip(t + w // 2, 0, T)
        cnt = (hi - lo).astype(jnp.float32)[None, :, None]
        mean = (cs[:, hi, g] - cs[:, lo, g]) / cnt
        outs.append(mean - pf[:, :, g])
    pooled = jnp.stack(outs, axis=2)
    mixed = jnp.einsum("btgc,gcd->btgd", pooled, w_pool.astype(jnp.float32))
    return (mixed.reshape(B, T, POOL_WIDTH) * scale.astype(jnp.float32)).astype(p.dtype)


def grouped_moe(h, router_w, router_bias, wg, wu, wd):
    B, T, D = h.shape
    x = h.reshape(B * T, D)
    aff = jax.nn.sigmoid(x.astype(jnp.float32) @ router_w.astype(jnp.float32))
    sel = aff + router_bias.astype(jnp.float32)
    grp = sel.reshape(-1, N_EXPERT_GROUPS, EXPERTS_PER_GROUP)
    grp_score = lax.top_k(grp, TOP_K)[0].sum(axis=-1)
    best_grp = jnp.argmax(grp_score, axis=-1)
    expert_mask = jnp.repeat(jax.nn.one_hot(best_grp, N_EXPERT_GROUPS, dtype=jnp.float32) > 0,
                             EXPERTS_PER_GROUP, axis=-1)
    masked = jnp.where(expert_mask, sel, -jnp.inf)
    _, idx = lax.top_k(masked, TOP_K)
    w_sel = jnp.take_along_axis(aff, idx, axis=-1)
    w_sel = w_sel / jnp.sum(w_sel, axis=-1, keepdims=True)
    gates = jnp.sum(jax.nn.one_hot(idx, N_EXPERTS, dtype=jnp.float32) * w_sel[..., None],
                    axis=1).astype(x.dtype)
    y = jnp.zeros_like(x)
    for e in range(N_EXPERTS):
        hid = jax.nn.silu(x @ wg[e]) * (x @ wu[e])
        y = y + gates[:, e:e + 1] * (hid @ wd[e])
    return y.reshape(B, T, D)


def setup_inputs(seed: int = 0) -> dict:
    key = jax.random.key(seed)
    ks = jax.random.split(key, 24)
    f32 = jnp.float32
    nrm = lambda k, shape, s: jax.random.normal(k, shape, f32) * s
    return {
        "x_prompt": nrm(ks[0], (BATCH, SEQ, D_MODEL), 1.0),
        "x_sample": nrm(ks[1], (DEC_BATCH, DEC_SEQ, D_MODEL), 1.0),
        "cache_k": nrm(ks[2], (DEC_BATCH, DEPTH, PAST_LEN, N_KV_HEADS, HEAD_DIM), 1.0),
        "cache_v": nrm(ks[3], (DEC_BATCH, DEPTH, PAST_LEN, N_KV_HEADS, HEAD_DIM), 1.0),
        "c": nrm(ks[4], (DEC_BATCH, D_MODEL), 1.0),
        "c_ctx": nrm(ks[5], (D_MODEL,), 1.0),
        "norm1_g": 1.0 + nrm(ks[6], (DEPTH, D_MODEL), 0.02),
        "norm2_g": 1.0 + nrm(ks[7], (DEPTH, D_MODEL), 0.02),
        "ada_w": nrm(ks[8], (DEPTH, D_MODEL, 6 * D_MODEL), 0.5 * D_MODEL ** -0.5),
        "ada_b": nrm(ks[9], (DEPTH, 6 * D_MODEL), 0.02),
        "w_in": nrm(ks[10], (DEPTH, D_MODEL, IN_PROJ_WIDTH), D_MODEL ** -0.5),
        "q_norm_g": 1.0 + nrm(ks[11], (DEPTH, HEAD_DIM), 0.02),
        "k_norm_g": 1.0 + nrm(ks[12], (DEPTH, HEAD_DIM), 0.02),
        "pool_w": nrm(ks[13], (DEPTH, N_POOL_GROUPS, POOL_GROUP_WIDTH, POOL_GROUP_WIDTH),
                       POOL_GROUP_WIDTH ** -0.5),
        "pool_scale": 1.0 + nrm(ks[14], (DEPTH, POOL_WIDTH), 0.1),
        "w_out": nrm(ks[15], (DEPTH, MIX_WIDTH, D_MODEL), MIX_WIDTH ** -0.5),
        "router_w": nrm(ks[16], (D_MODEL, N_EXPERTS), D_MODEL ** -0.5),
        "router_bias": nrm(ks[17], (N_EXPERTS,), 0.01),
        "expert_w_gate": nrm(ks[18], (DEPTH, N_EXPERTS, D_MODEL, D_EXPERT), D_MODEL ** -0.5),
        "expert_w_up": nrm(ks[19], (DEPTH, N_EXPERTS, D_MODEL, D_EXPERT), D_MODEL ** -0.5),
        "expert_w_down": nrm(ks[20], (DEPTH, N_EXPERTS, D_EXPERT, D_MODEL), D_EXPERT ** -0.5),
    }


def reference(x_prompt, x_sample, cache_k, cache_v, c, c_ctx, norm1_g, norm2_g, ada_w, ada_b,
              w_in, q_norm_g, k_norm_g, pool_w, pool_scale, w_out, router_w, router_bias,
              expert_w_gate, expert_w_up, expert_w_down):
    rows = x_sample.shape[1] // GRID_W
    row_idx = jnp.repeat(jnp.arange(rows, dtype=jnp.int32), GRID_W)
    col_idx = jnp.tile(jnp.arange(GRID_W, dtype=jnp.int32), rows)
    ang_r, ang_c = axial_angles(row_idx, col_idx)

    cond_ctx = jax.nn.silu(c_ctx[None, :])
    cond_lat = jax.nn.silu(c)

    xp = x_prompt
    xs = x_sample
    new_k = []
    new_v = []
    for l in range(DEPTH):
        mp = modulation(cond_ctx, ada_w[l], ada_b[l])
        ms = modulation(cond_lat, ada_w[l], ada_b[l])

        h = modulate(rms_norm(xp, norm1_g[l]), mp[0], mp[1])
        q, k, v, p = in_projection(h, w_in[l])
        q = rms_norm(q, q_norm_g[l])
        k = rms_norm(k, k_norm_g[l])
        att = blocked_gqa_attention(q, k, v)
        pool = multiscale_pool(p, pool_w[l], pool_scale[l])
        xp = xp + mp[2] * (jnp.concatenate([att, pool], axis=-1) @ w_out[l])
        h = modulate(rms_norm(xp, norm2_g[l]), mp[3], mp[4])
        xp = xp + mp[5] * grouped_moe(h, router_w, router_bias, expert_w_gate[l],
                                      expert_w_up[l], expert_w_down[l])
        new_k.append(k)
        new_v.append(v)

        h = modulate(rms_norm(xs, norm1_g[l]), ms[0], ms[1])
        q, k, v, p = in_projection(h, w_in[l])
        q = apply_axial_rope(rms_norm(q, q_norm_g[l]), ang_r, ang_c)
        k = apply_axial_rope(rms_norm(k, k_norm_g[l]), ang_r, ang_c)
        k_all = jnp.concatenate([cache_k[:, l].astype(k.dtype), k], axis=1)
        v_all = jnp.concatenate([cache_v[:, l].astype(v.dtype), v], axis=1)
        att = blocked_gqa_attention(q, k_all, v_all)
        pool = multiscale_pool(p, pool_w[l], pool_scale[l])
        xs = xs + ms[2] * (jnp.concatenate([att, pool], axis=-1) @ w_out[l])
        h = modulate(rms_norm(xs, norm2_g[l]), ms[3], ms[4])
        xs = xs + ms[5] * grouped_moe(h, router_w, router_bias, expert_w_gate[l],
                                      expert_w_up[l], expert_w_down[l])

    return (xp, xs, jnp.stack(new_k, axis=1), jnp.stack(new_v, axis=1))
```

```python
import functools

import jax
import jax.numpy as jnp
from jax import lax
from jax.experimental import pallas as pl
from jax.experimental.pallas import tpu as pltpu

D_MODEL = 1024
DEPTH = 4
GRID_W = 64
HEAD_DIM = 64
N_Q_HEADS = 8
N_KV_HEADS = 2
Q_PER_KV = N_Q_HEADS // N_KV_HEADS
ATTN_WIDTH = N_Q_HEADS * HEAD_DIM
KV_WIDTH = N_KV_HEADS * HEAD_DIM
QKV_WIDTH = ATTN_WIDTH + 2 * KV_WIDTH
POOL_WINDOWS = (2, 4, 8, 16)
POOL_WIDTH = D_MODEL - ATTN_WIDTH
POOL_GROUP_WIDTH = POOL_WIDTH // len(POOL_WINDOWS)
N_EXPERTS = 16
N_EXPERT_GROUPS = 4
EXPERTS_PER_GROUP = N_EXPERTS // N_EXPERT_GROUPS
D_EXPERT = 512
ROPE_THETA = 10000.0
EPS = 1e-6

MOD_ROWS = 16
POOL_HALO = 16
V7X_LANES = 128
VMEM_LIMIT = 48 * 1024 * 1024

F32 = jnp.float32
BF16 = jnp.bfloat16
NT_DIMS = (((1,), (1,)), ((), ()))
TN_DIMS = (((0,), (0,)), ((), ()))


def _params(sem):
    return pltpu.CompilerParams(dimension_semantics=sem, vmem_limit_bytes=VMEM_LIMIT)


def _mod_kernel(c_ref, w_ref, b_ref, o_ref):
    c = c_ref[...]
    a = c * jax.nn.sigmoid(c)
    o_ref[0] = jnp.dot(a, w_ref[0], preferred_element_type=F32,
                       precision=lax.Precision.HIGHEST) + b_ref[0]


def _modulation(cond, ada_w, ada_b):
    tn = 1536
    n_out = ada_w.shape[2]
    return pl.pallas_call(
        _mod_kernel,
        grid=(DEPTH, n_out // tn),
        in_specs=[
            pl.BlockSpec((MOD_ROWS, D_MODEL), lambda l, n: (0, 0)),
            pl.BlockSpec((1, D_MODEL, tn), lambda l, n: (l, 0, n)),
            pl.BlockSpec((1, 1, tn), lambda l, n: (l, 0, n)),
        ],
        out_specs=pl.BlockSpec((1, MOD_ROWS, tn), lambda l, n: (l, 0, n)),
        out_shape=jax.ShapeDtypeStruct((DEPTH, MOD_ROWS, n_out), F32),
        compiler_params=_params(("arbitrary", "arbitrary")),
        name="modulation",
    )(cond, ada_w, ada_b.reshape(DEPTH, 1, n_out))


def _rms_modulate(x, g, shift, scale):
    ms = jnp.mean(x * x, axis=-1, keepdims=True)
    h = x * lax.rsqrt(ms + EPS) * g
    return h * (1.0 + scale) + shift


def _pre_kernel(x_ref, mod_ref, g1_ref, wqkv_ref, wp_ref, gq_ref, gk_ref, *rest, rope, emit_f32):
    if rope:
        cr_ref, sr_ref, cc_ref, sc_ref = rest[:4]
        rest = rest[4:]
    qT_ref, k_ref, vT_ref, p_ref = rest[:4]
    tm = x_ref.shape[1]
    m = mod_ref[0]
    h = _rms_modulate(x_ref[0], g1_ref[...], m[:, 0:D_MODEL], m[:, D_MODEL:2 * D_MODEL])
    hb = h.astype(BF16)
    zT = lax.dot_general(wqkv_ref[...], hb, NT_DIMS, preferred_element_type=F32)
    p_ref[0] = jnp.dot(hb, wp_ref[...], preferred_element_type=F32).astype(BF16)

    def norm_rope(zt, n_heads, g_ref):
        z3 = zt.reshape(n_heads, HEAD_DIM, tm)
        ms = jnp.mean(z3 * z3, axis=1, keepdims=True)
        y = z3 * lax.rsqrt(ms + EPS) * g_ref[...][None]
        if rope:
            cr, sr = cr_ref[...][None], sr_ref[...][None]
            cc, sc = cc_ref[...][None], sc_ref[...][None]
            x1r, x2r, x1c, x2c = y[:, 0:16], y[:, 16:32], y[:, 32:48], y[:, 48:64]
            y = jnp.concatenate([x1r * cr - x2r * sr, x1r * sr + x2r * cr,
                                 x1c * cc - x2c * sc, x1c * sc + x2c * cc], axis=1)
        return y.reshape(n_heads * HEAD_DIM, tm)

    qT = norm_rope(zT[0:ATTN_WIDTH], N_Q_HEADS, gq_ref)
    qT_ref[0] = (qT * (HEAD_DIM ** -0.5)).astype(BF16)
    kT = norm_rope(zT[ATTN_WIDTH:ATTN_WIDTH + KV_WIDTH], N_KV_HEADS, gk_ref)
    k_tok = kT.T
    k_ref[0] = k_tok.astype(BF16)
    vT = zT[ATTN_WIDTH + KV_WIDTH:QKV_WIDTH]
    vT_ref[0] = vT.astype(BF16)
    if emit_f32:
        k32_ref, v32_ref = rest[4:6]
        k32_ref[0] = k_tok
        v32_ref[0] = vT.T


def _pre(x, mod_l, row0, g1, wqkvT, wp, gq, gk, rope_tabs, tm, emit_f32):
    b, t, _ = x.shape
    rope = rope_tabs is not None
    in_specs = [
        pl.BlockSpec((1, tm, D_MODEL), lambda bi, i: (bi, i, 0)),
        pl.BlockSpec((1, 1, 6 * D_MODEL),
                     (lambda bi, i: (row0 + bi, 0, 0)) if rope else (lambda bi, i: (row0, 0, 0))),
        pl.BlockSpec((1, D_MODEL), lambda bi, i: (0, 0)),
        pl.BlockSpec((QKV_WIDTH, D_MODEL), lambda bi, i: (0, 0)),
        pl.BlockSpec((D_MODEL, POOL_WIDTH), lambda bi, i: (0, 0)),
        pl.BlockSpec((HEAD_DIM, 1), lambda bi, i: (0, 0)),
        pl.BlockSpec((HEAD_DIM, 1), lambda bi, i: (0, 0)),
    ]
    args = [x, mod_l, g1, wqkvT, wp, gq, gk]
    if rope:
        in_specs += [pl.BlockSpec((16, tm), lambda bi, i: (0, i))] * 4
        args += list(rope_tabs)
    out_shape = [
        jax.ShapeDtypeStruct((b, ATTN_WIDTH, t), BF16),
        jax.ShapeDtypeStruct((b, t, KV_WIDTH), BF16),
        jax.ShapeDtypeStruct((b, KV_WIDTH, t), BF16),
        jax.ShapeDtypeStruct((b, t, POOL_WIDTH), BF16),
    ]
    out_specs = [
        pl.BlockSpec((1, ATTN_WIDTH, tm), lambda bi, i: (bi, 0, i)),
        pl.BlockSpec((1, tm, KV_WIDTH), lambda bi, i: (bi, i, 0)),
        pl.BlockSpec((1, KV_WIDTH, tm), lambda bi, i: (bi, 0, i)),
        pl.BlockSpec((1, tm, POOL_WIDTH), lambda bi, i: (bi, i, 0)),
    ]
    if emit_f32:
        out_shape += [jax.ShapeDtypeStruct((b, t, KV_WIDTH), F32)] * 2
        out_specs += [pl.BlockSpec((1, tm, KV_WIDTH), lambda bi, i: (bi, i, 0))] * 2
    return pl.pallas_call(
        functools.partial(_pre_kernel, rope=rope, emit_f32=emit_f32),
        grid=(b, t // tm),
        in_specs=in_specs,
        out_specs=out_specs,
        out_shape=out_shape,
        compiler_params=_params(("parallel", "parallel")),
        name="pre_attention",
    )(*args)


def _attn_kernel(qT_ref, k_ref, vT_ref, o_ref, qpad_ref, acc_ref, m_ref, l_ref, *, tk):
    tq = qT_ref.shape[2]
    s_len = k_ref.shape[1]
    zeros = jnp.zeros((HEAD_DIM, tq), BF16)
    for h in range(N_Q_HEADS):
        qh = qT_ref[0, h * HEAD_DIM:(h + 1) * HEAD_DIM, :]
        qpad_ref[h] = jnp.concatenate([qh, zeros] if h < Q_PER_KV else [zeros, qh], axis=0)
    m_ref[...] = jnp.full(m_ref.shape, -jnp.inf, F32)
    l_ref[...] = jnp.zeros(l_ref.shape, F32)
    acc_ref[...] = jnp.zeros(acc_ref.shape, F32)

    def step(t, carry):
        off = pl.multiple_of(t * tk, tk)
        kb = k_ref[0, pl.ds(off, tk), :]
        vb = vT_ref[0, :, pl.ds(off, tk)]
        for h in range(N_Q_HEADS):
            j = h // Q_PER_KV
            s = jnp.dot(kb, qpad_ref[h], preferred_element_type=F32)
            m_old = m_ref[h]
            m_new = jnp.maximum(m_old, jnp.max(s, axis=0, keepdims=True))
            alpha = jnp.exp(m_old - m_new)
            p = jnp.exp(s - m_new)
            l_ref[h] = alpha * l_ref[h] + jnp.sum(p, axis=0, keepdims=True)
            pv = jnp.dot(vb[j * HEAD_DIM:(j + 1) * HEAD_DIM, :], p.astype(BF16),
                         preferred_element_type=F32)
            acc_ref[h] = alpha * acc_ref[h] + pv
            m_ref[h] = m_new
        return carry

    lax.fori_loop(0, s_len // tk, step, 0)
    for h in range(N_Q_HEADS):
        o_ref[0, h * HEAD_DIM:(h + 1) * HEAD_DIM, :] = (acc_ref[h] / l_ref[h]).astype(BF16)


def _attention(qT, k_all, vT_all, tq, tk):
    b, _, t = qT.shape
    s_len = k_all.shape[1]
    return pl.pallas_call(
        functools.partial(_attn_kernel, tk=tk),
        grid=(b, t // tq),
        in_specs=[
            pl.BlockSpec((1, ATTN_WIDTH, tq), lambda bi, i: (bi, 0, i)),
            pl.BlockSpec((1, s_len, KV_WIDTH), lambda bi, i: (bi, 0, 0)),
            pl.BlockSpec((1, KV_WIDTH, s_len), lambda bi, i: (bi, 0, 0)),
        ],
        out_specs=pl.BlockSpec((1, ATTN_WIDTH, tq), lambda bi, i: (bi, 0, i)),
        out_shape=jax.ShapeDtypeStruct((b, ATTN_WIDTH, t), BF16),
        scratch_shapes=[
            pltpu.VMEM((N_Q_HEADS, 2 * HEAD_DIM, tq), BF16),
            pltpu.VMEM((N_Q_HEADS, HEAD_DIM, tq), F32),
            pltpu.VMEM((N_Q_HEADS, 1, tq), F32),
            pltpu.VMEM((N_Q_HEADS, 1, tq), F32),
        ],
        compiler_params=_params(("parallel", "parallel")),
        name="attention",
    )(qT, k_all, vT_all)


def _route(logits, bias):
    aff = jax.nn.sigmoid(logits)
    sel = aff + bias
    eidx = lax.broadcasted_iota(jnp.int32, sel.shape, 0).astype(F32)
    scores = []
    for g in range(N_EXPERT_GROUPS):
        r = [sel[EXPERTS_PER_GROUP * g + i:EXPERTS_PER_GROUP * g + i + 1, :]
             for i in range(EXPERTS_PER_GROUP)]
        best = None
        for i in range(EXPERTS_PER_GROUP):
            for j in range(i + 1, EXPERTS_PER_GROUP):
                sij = r[i] + r[j]
                best = sij if best is None else jnp.maximum(best, sij)
        scores.append(best)
    gmax = functools.reduce(jnp.maximum, scores)
    gstar = jnp.where(scores[0] == gmax, 0.0,
                      jnp.where(scores[1] == gmax, 1.0, jnp.where(scores[2] == gmax, 2.0, 3.0)))
    in_grp = jnp.floor(eidx * (1.0 / EXPERTS_PER_GROUP)) == gstar
    cand = jnp.where(in_grp, sel, -jnp.inf)
    m1 = jnp.max(cand, axis=0, keepdims=True)
    idx1 = jnp.min(jnp.where(cand == m1, eidx, float(N_EXPERTS)), axis=0, keepdims=True)
    cand2 = jnp.where(eidx == idx1, -jnp.inf, cand)
    m2 = jnp.max(cand2, axis=0, keepdims=True)
    idx2 = jnp.min(jnp.where(cand2 == m2, eidx, float(N_EXPERTS)), axis=0, keepdims=True)
    a1 = jnp.sum(jnp.where(eidx == idx1, aff, 0.0), axis=0, keepdims=True)
    a2 = jnp.sum(jnp.where(eidx == idx2, aff, 0.0), axis=0, keepdims=True)
    den = a1 + a2
    return jnp.where(eidx == idx1, a1 / den, 0.0) + jnp.where(eidx == idx2, a2 / den, 0.0)


def _post_kernel(x_ref, aT_ref, p_ref, pprev_ref, pnext_ref, mod_ref, wot_ref, wob_ref, pw_ref,
                 ps_ref, g2_ref, wr_ref, rb_ref, xmid_ref, h2_ref, gT_ref, *, seq_len):
    tm = x_ref.shape[1]
    i = pl.program_id(1)
    n_tiles = seq_len // tm
    m = mod_ref[0]
    gate_msa = m[:, 2 * D_MODEL:3 * D_MODEL]
    shift2 = m[:, 3 * D_MODEL:4 * D_MODEL]
    scale2 = m[:, 4 * D_MODEL:5 * D_MODEL]

    pc = p_ref[0].astype(F32)
    prev = jnp.where(i > 0, pprev_ref[0].astype(F32), 0.0)
    nxt = jnp.where(i < n_tiles - 1, pnext_ref[0].astype(F32), 0.0)
    ext = jnp.concatenate([prev, pc, nxt], axis=0)
    t_seq = lax.broadcasted_iota(jnp.int32, (tm, POOL_GROUP_WIDTH), 0) + i * tm
    pools = []
    for g, w in enumerate(POOL_WINDOWS):
        lanes = slice(g * POOL_GROUP_WIDTH, (g + 1) * POOL_GROUP_WIDTH)
        a = ext[:, lanes]
        span = 1
        while span < w:
            a = a[:-span] + a[span:]
            span *= 2
        start = POOL_HALO - w // 2
        wsum = a[start:start + tm]
        lo = jnp.maximum(t_seq - w // 2, 0)
        hi = jnp.minimum(t_seq + w // 2, seq_len)
        pooled = wsum / (hi - lo).astype(F32) - pc[:, lanes]
        mixed = jnp.dot(pooled.astype(BF16), pw_ref[g], preferred_element_type=F32)
        pools.append(mixed * ps_ref[:, lanes])
    pool = jnp.concatenate(pools, axis=1).astype(BF16)

    proj = lax.dot_general(aT_ref[0], wot_ref[...], TN_DIMS, preferred_element_type=F32)
    proj = proj + jnp.dot(pool, wob_ref[...], preferred_element_type=F32)
    xm = x_ref[0] + gate_msa * proj
    xmid_ref[0] = xm

    h2 = _rms_modulate(xm, g2_ref[...], shift2, scale2)
    h_hi = h2.astype(BF16)
    h2_ref[0] = h_hi
    h_lo = (h2 - h_hi.astype(F32)).astype(BF16)
    r = lax.dot_general(wr_ref[...], jnp.concatenate([h_hi, h_lo], axis=0), NT_DIMS,
                        preferred_element_type=F32)
    logits = r[0:N_EXPERTS, 0:tm] + r[0:N_EXPERTS, tm:2 * tm] + r[N_EXPERTS:2 * N_EXPERTS, 0:tm]
    gT_ref[0] = _route(logits, rb_ref[...])


def _post(x, aT, p, mod_l, row0, per_batch_mod, wo_top, wo_bot, pool_w, pool_scale, g2, wr2, rb, tm):
    b, t, _ = x.shape
    hb = tm // POOL_HALO
    last = t // POOL_HALO - 1
    mod_map = (lambda bi, i: (row0 + bi, 0, 0)) if per_batch_mod else (lambda bi, i: (row0, 0, 0))
    const2 = lambda bi, i: (0, 0)
    return pl.pallas_call(
        functools.partial(_post_kernel, seq_len=t),
        grid=(b, t // tm),
        in_specs=[
            pl.BlockSpec((1, tm, D_MODEL), lambda bi, i: (bi, i, 0)),
            pl.BlockSpec((1, ATTN_WIDTH, tm), lambda bi, i: (bi, 0, i)),
            pl.BlockSpec((1, tm, POOL_WIDTH), lambda bi, i: (bi, i, 0)),
            pl.BlockSpec((1, POOL_HALO, POOL_WIDTH), lambda bi, i: (bi, jnp.maximum(i * hb - 1, 0), 0)),
            pl.BlockSpec((1, POOL_HALO, POOL_WIDTH), lambda bi, i: (bi, jnp.minimum((i + 1) * hb, last), 0)),
            pl.BlockSpec((1, 1, 6 * D_MODEL), mod_map),
            pl.BlockSpec((ATTN_WIDTH, D_MODEL), const2),
            pl.BlockSpec((POOL_WIDTH, D_MODEL), const2),
            pl.BlockSpec((len(POOL_WINDOWS), POOL_GROUP_WIDTH, POOL_GROUP_WIDTH), lambda bi, i: (0, 0, 0)),
            pl.BlockSpec((1, POOL_WIDTH), const2),
            pl.BlockSpec((1, D_MODEL), const2),
            pl.BlockSpec((2 * N_EXPERTS, D_MODEL), const2),
            pl.BlockSpec((N_EXPERTS, 1), const2),
        ],
        out_specs=[
            pl.BlockSpec((1, tm, D_MODEL), lambda bi, i: (bi, i, 0)),
            pl.BlockSpec((1, tm, D_MODEL), lambda bi, i: (bi, i, 0)),
            pl.BlockSpec((1, N_EXPERTS, tm), lambda bi, i: (bi, 0, i)),
        ],
        out_shape=[
            jax.ShapeDtypeStruct((b, t, D_MODEL), F32),
            jax.ShapeDtypeStruct((b, t, D_MODEL), BF16),
            jax.ShapeDtypeStruct((b, N_EXPERTS, t), F32),
        ],
        compiler_params=_params(("parallel", "parallel")),
        name="post_attention",
    )(x, aT, p, p, p, mod_l, wo_top, wo_bot, pool_w, pool_scale, g2, wr2, rb)


def _moe_kernel(h_ref, xmid_ref, gT_ref, mod_ref, wg_ref, wu_ref, wd_ref, o_ref):
    e = pl.program_id(2)
    tm = h_ref.shape[1]

    @pl.when(e == 0)
    def _():
        o_ref[0] = xmid_ref[0]

    h = h_ref[0]
    a = jnp.dot(h, wg_ref[0], preferred_element_type=F32)
    u = jnp.dot(h, wu_ref[0], preferred_element_type=F32)
    hid = (a * jax.nn.sigmoid(a)) * u
    y = jnp.dot(hid.astype(BF16), wd_ref[0], preferred_element_type=F32)
    grow = gT_ref[0, pl.ds(e, 1), :]
    gcol = jnp.broadcast_to(grow, (V7X_LANES, tm)).T
    gate_mlp = mod_ref[0][:, 5 * D_MODEL:6 * D_MODEL]
    gfull = jnp.concatenate([gcol] * (D_MODEL // V7X_LANES), axis=1)
    o_ref[0] += gate_mlp * (gfull * y)


def _moe(h2, xmid, gT, mod_l, row0, per_batch_mod, wg, wu, wd, tm):
    b, t, _ = h2.shape
    mod_map = (lambda bi, i, e: (row0 + bi, 0, 0)) if per_batch_mod else (lambda bi, i, e: (row0, 0, 0))
    return pl.pallas_call(
        _moe_kernel,
        grid=(b, t // tm, N_EXPERTS),
        in_specs=[
            pl.BlockSpec((1, tm, D_MODEL), lambda bi, i, e: (bi, i, 0)),
            pl.BlockSpec((1, tm, D_MODEL), lambda bi, i, e: (bi, i, 0)),
            pl.BlockSpec((1, N_EXPERTS, tm), lambda bi, i, e: (bi, 0, i)),
            pl.BlockSpec((1, 1, 6 * D_MODEL), mod_map),
            pl.BlockSpec((1, D_MODEL, D_EXPERT), lambda bi, i, e: (e, 0, 0)),
            pl.BlockSpec((1, D_MODEL, D_EXPERT), lambda bi, i, e: (e, 0, 0)),
            pl.BlockSpec((1, D_EXPERT, D_MODEL), lambda bi, i, e: (e, 0, 0)),
        ],
        out_specs=pl.BlockSpec((1, tm, D_MODEL), lambda bi, i, e: (bi, i, 0)),
        out_shape=jax.ShapeDtypeStruct((b, t, D_MODEL), F32),
        compiler_params=_params(("parallel", "parallel", "arbitrary")),
        name="experts",
    )(h2, xmid, gT, mod_l, wg, wu, wd)


def _rope_tables(t):
    half = HEAD_DIM // 2
    inv_freq = ROPE_THETA ** (-jnp.arange(0, half, 2, dtype=F32) / half)
    pos = jnp.arange(t, dtype=jnp.int32)
    ang_r = (pos // GRID_W).astype(F32)[None, :] * inv_freq[:, None]
    ang_c = (pos % GRID_W).astype(F32)[None, :] * inv_freq[:, None]
    return jnp.cos(ang_r), jnp.sin(ang_r), jnp.cos(ang_c), jnp.sin(ang_c)


def kernel(x_prompt, x_sample, cache_k, cache_v, c, c_ctx, norm1_g, norm2_g, ada_w, ada_b, w_in,
           q_norm_g, k_norm_g, pool_w, pool_scale, w_out, router_w, router_bias, expert_w_gate,
           expert_w_up, expert_w_down):
    n_ctx, t_ctx, _ = x_prompt.shape
    n_lat, t_lat, _ = x_sample.shape
    past = cache_k.shape[2]

    cond = jnp.concatenate([c_ctx[None, :], c,
                            jnp.zeros((MOD_ROWS - 1 - n_lat, D_MODEL), F32)], axis=0)
    mod = _modulation(cond, ada_w, ada_b).reshape(DEPTH, MOD_ROWS, 1, 6 * D_MODEL)
    rope_tabs = _rope_tables(t_lat)

    wrT = router_w.T
    wr_hi = wrT.astype(BF16)
    wr2 = jnp.concatenate([wr_hi, (wrT - wr_hi.astype(F32)).astype(BF16)], axis=0)
    rb = router_bias.reshape(N_EXPERTS, 1)

    xp, xs = x_prompt, x_sample
    new_k, new_v = [], []
    for l in range(DEPTH):
        w_in_b = w_in[l].astype(BF16)
        wqkvT = w_in_b[:, :QKV_WIDTH].T
        wp = w_in_b[:, QKV_WIDTH:]
        wo = w_out[l].astype(BF16)
        wo_top, wo_bot = wo[:ATTN_WIDTH], wo[ATTN_WIDTH:]
        pw = pool_w[l].astype(BF16)
        ps = pool_scale[l].reshape(1, POOL_WIDTH)
        g1 = norm1_g[l].reshape(1, D_MODEL)
        g2 = norm2_g[l].reshape(1, D_MODEL)
        gq = q_norm_g[l].reshape(HEAD_DIM, 1)
        gk = k_norm_g[l].reshape(HEAD_DIM, 1)
        wg = expert_w_gate[l].astype(BF16)
        wu = expert_w_up[l].astype(BF16)
        wd = expert_w_down[l].astype(BF16)
        mod_l = mod[l]

        qT, k, vT, p, k32, v32 = _pre(xp, mod_l, 0, g1, wqkvT, wp, gq, gk, None, t_ctx, True)
        aT = _attention(qT, k, vT, t_ctx, t_ctx)
        xmid, h2, gT = _post(xp, aT, p, mod_l, 0, False, wo_top, wo_bot, pw, ps, g2, wr2, rb, t_ctx)
        xp = _moe(h2, xmid, gT, mod_l, 0, False, wg, wu, wd, t_ctx)
        new_k.append(k32.reshape(n_ctx, t_ctx, N_KV_HEADS, HEAD_DIM))
        new_v.append(v32.reshape(n_ctx, t_ctx, N_KV_HEADS, HEAD_DIM))

        qT, k, vT, p = _pre(xs, mod_l, 1, g1, wqkvT, wp, gq, gk, rope_tabs, 512, False)
        ck = cache_k[:, l].reshape(n_lat, past, KV_WIDTH).astype(BF16)
        cvT = cache_v[:, l].reshape(n_lat, past, KV_WIDTH).astype(BF16).transpose(0, 2, 1)
        k_all = jnp.concatenate([ck, k], axis=1)
        vT_all = jnp.concatenate([cvT, vT], axis=2)
        aT = _attention(qT, k_all, vT_all, 256, 512)
        xmid, h2, gT = _post(xs, aT, p, mod_l, 1, True, wo_top, wo_bot, pw, ps, g2, wr2, rb, 256)
        xs = _moe(h2, xmid, gT, mod_l, 1, True, wg, wu, wd, 512)

    return xp, xs, jnp.stack(new_k, axis=1), jnp.stack(new_v, axis=1)
```

```python
import functools

import jax
import jax.numpy as jnp
from jax import lax
from jax.experimental import pallas as pl
from jax.experimental.pallas import tpu as pltpu

D_MODEL = 1024
DEPTH = 4
GRID_W = 64
HEAD_DIM = 64
N_Q_HEADS = 8
N_KV_HEADS = 2
Q_PER_KV = N_Q_HEADS // N_KV_HEADS
ATTN_WIDTH = N_Q_HEADS * HEAD_DIM
KV_WIDTH = N_KV_HEADS * HEAD_DIM
QKV_WIDTH = ATTN_WIDTH + 2 * KV_WIDTH
POOL_WINDOWS = (2, 4, 8, 16)
POOL_WIDTH = D_MODEL - ATTN_WIDTH
POOL_GROUP_WIDTH = POOL_WIDTH // len(POOL_WINDOWS)
N_EXPERTS = 16
N_EXPERT_GROUPS = 4
EXPERTS_PER_GROUP = N_EXPERTS // N_EXPERT_GROUPS
D_EXPERT = 512
ROPE_THETA = 10000.0
EPS = 1e-6

Q_SCALE = HEAD_DIM ** -0.5 * 1.4426950408889634
DENOM_ROWS = 16
MOD_ROWS = 16
POOL_HALO = 16
V7X_LANES = 128
VMEM_LIMIT = 48 * 1024 * 1024

F32 = jnp.float32
BF16 = jnp.bfloat16
NT_DIMS = (((1,), (1,)), ((), ()))
TN_DIMS = (((0,), (0,)), ((), ()))


def _params(sem):
    return pltpu.CompilerParams(dimension_semantics=sem, vmem_limit_bytes=VMEM_LIMIT)


def _mod_kernel(c_ref, w_ref, b_ref, o_ref):
    c = c_ref[...]
    a = c * jax.nn.sigmoid(c)
    o_ref[0] = jnp.dot(a, w_ref[0], preferred_element_type=F32,
                       precision=lax.Precision.HIGHEST) + b_ref[0]


def _modulation(cond, ada_w, ada_b):
    tn = 1536
    n_out = ada_w.shape[2]
    return pl.pallas_call(
        _mod_kernel,
        grid=(DEPTH, n_out // tn),
        in_specs=[
            pl.BlockSpec((MOD_ROWS, D_MODEL), lambda l, n: (0, 0)),
            pl.BlockSpec((1, D_MODEL, tn), lambda l, n: (l, 0, n)),
            pl.BlockSpec((1, 1, tn), lambda l, n: (l, 0, n)),
        ],
        out_specs=pl.BlockSpec((1, MOD_ROWS, tn), lambda l, n: (l, 0, n)),
        out_shape=jax.ShapeDtypeStruct((DEPTH, MOD_ROWS, n_out), F32),
        compiler_params=_params(("arbitrary", "arbitrary")),
        name="modulation",
    )(cond, ada_w, ada_b.reshape(DEPTH, 1, n_out))


def _rms_modulate(x, g, shift, scale):
    ms = jnp.mean(x * x, axis=-1, keepdims=True)
    h = x * lax.rsqrt(ms + EPS) * g
    return h * (1.0 + scale) + shift


def _pre_kernel(x_ref, mod_ref, g1_ref, wqkv_ref, wp_ref, gq_ref, gk_ref, *rest, rope, emit_f32):
    if rope:
        cr_ref, sr_ref, cc_ref, sc_ref = rest[:4]
        rest = rest[4:]
    qT_ref, k_ref, vT_ref, p_ref = rest[:4]
    tm = x_ref.shape[1]
    m = mod_ref[0]
    h = _rms_modulate(x_ref[0], g1_ref[...], m[:, 0:D_MODEL], m[:, D_MODEL:2 * D_MODEL])
    hb = h.astype(BF16)
    zT = lax.dot_general(wqkv_ref[...], hb, NT_DIMS, preferred_element_type=F32)
    p_ref[0] = jnp.dot(hb, wp_ref[...], preferred_element_type=F32).astype(BF16)

    def norm_rope(zt, n_heads, g_ref):
        z3 = zt.reshape(n_heads, HEAD_DIM, tm)
        ms = jnp.mean(z3 * z3, axis=1, keepdims=True)
        y = z3 * lax.rsqrt(ms + EPS) * g_ref[...][None]
        if rope:
            cr, sr = cr_ref[...][None], sr_ref[...][None]
            cc, sc = cc_ref[...][None], sc_ref[...][None]
            x1r, x2r, x1c, x2c = y[:, 0:16], y[:, 16:32], y[:, 32:48], y[:, 48:64]
            y = jnp.concatenate([x1r * cr - x2r * sr, x1r * sr + x2r * cr,
                                 x1c * cc - x2c * sc, x1c * sc + x2c * cc], axis=1)
        return y.reshape(n_heads * HEAD_DIM, tm)

    qT = norm_rope(zT[0:ATTN_WIDTH], N_Q_HEADS, gq_ref)
    qT_ref[0] = (qT * Q_SCALE).astype(BF16)
    kT = norm_rope(zT[ATTN_WIDTH:ATTN_WIDTH + KV_WIDTH], N_KV_HEADS, gk_ref)
    k_tok = kT.T
    k_ref[0] = k_tok.astype(BF16)
    vT = zT[ATTN_WIDTH + KV_WIDTH:QKV_WIDTH]
    vT_ref[0] = vT.astype(BF16)
    if emit_f32:
        k32_ref, v32_ref = rest[4:6]
        k32_ref[0] = k_tok
        v32_ref[0] = vT.T


def _pre(x, mod_l, row0, g1, wqkvT, wp, gq, gk, rope_tabs, tm, emit_f32):
    b, t, _ = x.shape
    rope = rope_tabs is not None
    in_specs = [
        pl.BlockSpec((1, tm, D_MODEL), lambda bi, i: (bi, i, 0)),
        pl.BlockSpec((1, 1, 6 * D_MODEL),
                     (lambda bi, i: (row0 + bi, 0, 0)) if rope else (lambda bi, i: (row0, 0, 0))),
        pl.BlockSpec((1, D_MODEL), lambda bi, i: (0, 0)),
        pl.BlockSpec((QKV_WIDTH, D_MODEL), lambda bi, i: (0, 0)),
        pl.BlockSpec((D_MODEL, POOL_WIDTH), lambda bi, i: (0, 0)),
        pl.BlockSpec((HEAD_DIM, 1), lambda bi, i: (0, 0)),
        pl.BlockSpec((HEAD_DIM, 1), lambda bi, i: (0, 0)),
    ]
    args = [x, mod_l, g1, wqkvT, wp, gq, gk]
    if rope:
        in_specs += [pl.BlockSpec((16, tm), lambda bi, i: (0, i))] * 4
        args += list(rope_tabs)
    out_shape = [
        jax.ShapeDtypeStruct((b, ATTN_WIDTH, t), BF16),
        jax.ShapeDtypeStruct((b, t, KV_WIDTH), BF16),
        jax.ShapeDtypeStruct((b, KV_WIDTH, t), BF16),
        jax.ShapeDtypeStruct((b, t, POOL_WIDTH), BF16),
    ]
    out_specs = [
        pl.BlockSpec((1, ATTN_WIDTH, tm), lambda bi, i: (bi, 0, i)),
        pl.BlockSpec((1, tm, KV_WIDTH), lambda bi, i: (bi, i, 0)),
        pl.BlockSpec((1, KV_WIDTH, tm), lambda bi, i: (bi, 0, i)),
        pl.BlockSpec((1, tm, POOL_WIDTH), lambda bi, i: (bi, i, 0)),
    ]
    if emit_f32:
        out_shape += [jax.ShapeDtypeStruct((b, t, KV_WIDTH), F32)] * 2
        out_specs += [pl.BlockSpec((1, tm, KV_WIDTH), lambda bi, i: (bi, i, 0))] * 2
    return pl.pallas_call(
        functools.partial(_pre_kernel, rope=rope, emit_f32=emit_f32),
        grid=(b, t // tm),
        in_specs=in_specs,
        out_specs=out_specs,
        out_shape=out_shape,
        compiler_params=_params(("parallel", "parallel")),
        name="pre_attention",
    )(*args)


def _attn_kernel(qT_ref, k_ref, vT_ref, o_ref, qpad_ref, acc_ref, m_ref, *, tk, unroll):
    tq = qT_ref.shape[2]
    s_len = k_ref.shape[1]
    n_pairs = N_Q_HEADS // 2
    zeros = jnp.zeros((HEAD_DIM, tq), BF16)
    for h in range(N_Q_HEADS):
        qh = qT_ref[0, h * HEAD_DIM:(h + 1) * HEAD_DIM, :]
        qpad_ref[h // 2, :, (h % 2) * tq:(h % 2 + 1) * tq] = jnp.concatenate(
            [qh, zeros] if h < Q_PER_KV else [zeros, qh], axis=0)
    m_ref[...] = jnp.full(m_ref.shape, -jnp.inf, F32)
    acc_ref[...] = jnp.zeros(acc_ref.shape, F32)

    n_items = unroll * n_pairs

    def step(t, carry):
        ones = jnp.ones((DENOM_ROWS, tk), BF16)

        def keys(u):
            off = pl.multiple_of((t * unroll + u) * tk, tk)
            return k_ref[0, pl.ds(off, tk), :]

        def values(u, j):
            off = pl.multiple_of((t * unroll + u) * tk, tk)
            vb = vT_ref[0, j * HEAD_DIM:(j + 1) * HEAD_DIM, pl.ds(off, tk)]
            return jnp.concatenate([vb, ones], axis=0)

        def scores(i):
            return jnp.dot(keys(i // n_pairs), qpad_ref[i % n_pairs],
                           preferred_element_type=F32)

        queue = [scores(0), scores(1)]
        for i in range(n_items):
            u, pr = divmod(i, n_pairs)
            j = (2 * pr) // Q_PER_KV
            s = queue.pop(0)
            m_old = m_ref[pr]
            m_new = jnp.maximum(m_old, jnp.max(s, axis=0, keepdims=True))
            alpha = jnp.exp2(m_old - m_new)
            p = jnp.exp2(s - m_new).astype(BF16)
            if i + 2 < n_items:
                queue.append(scores(i + 2))
            pv = jnp.dot(values(u, j), p, preferred_element_type=F32)
            acc_ref[pr] = alpha * acc_ref[pr] + pv
            m_ref[pr] = m_new
        return carry

    lax.fori_loop(0, s_len // (tk * unroll), step, 0)
    for h in range(N_Q_HEADS):
        lanes = slice((h % 2) * tq, (h % 2 + 1) * tq)
        o_ref[0, h * HEAD_DIM:(h + 1) * HEAD_DIM, :] = (
            acc_ref[h // 2, 0:HEAD_DIM, lanes]
            / acc_ref[h // 2, HEAD_DIM:HEAD_DIM + 1, lanes]).astype(BF16)


def _attention(qT, k_all, vT_all, tq, tk, unroll):
    b, _, t = qT.shape
    s_len = k_all.shape[1]
    return pl.pallas_call(
        functools.partial(_attn_kernel, tk=tk, unroll=unroll),
        grid=(b, t // tq),
        in_specs=[
            pl.BlockSpec((1, ATTN_WIDTH, tq), lambda bi, i: (bi, 0, i)),
            pl.BlockSpec((1, s_len, KV_WIDTH), lambda bi, i: (bi, 0, 0)),
            pl.BlockSpec((1, KV_WIDTH, s_len), lambda bi, i: (bi, 0, 0)),
        ],
        out_specs=pl.BlockSpec((1, ATTN_WIDTH, tq), lambda bi, i: (bi, 0, i)),
        out_shape=jax.ShapeDtypeStruct((b, ATTN_WIDTH, t), BF16),
        scratch_shapes=[
            pltpu.VMEM((N_Q_HEADS // 2, 2 * HEAD_DIM, 2 * tq), BF16),
            pltpu.VMEM((N_Q_HEADS // 2, HEAD_DIM + DENOM_ROWS, 2 * tq), F32),
            pltpu.VMEM((N_Q_HEADS // 2, 1, 2 * tq), F32),
        ],
        compiler_params=_params(("parallel", "parallel")),
        name="attention",
    )(qT, k_all, vT_all)


def _route(logits, bias):
    aff = jax.nn.sigmoid(logits)
    sel = aff + bias
    eidx = lax.broadcasted_iota(jnp.int32, sel.shape, 0).astype(F32)
    scores = []
    for g in range(N_EXPERT_GROUPS):
        r = [sel[EXPERTS_PER_GROUP * g + i:EXPERTS_PER_GROUP * g + i + 1, :]
             for i in range(EXPERTS_PER_GROUP)]
        best = None
        for i in range(EXPERTS_PER_GROUP):
            for j in range(i + 1, EXPERTS_PER_GROUP):
                sij = r[i] + r[j]
                best = sij if best is None else jnp.maximum(best, sij)
        scores.append(best)
    gmax = functools.reduce(jnp.maximum, scores)
    gstar = jnp.where(scores[0] == gmax, 0.0,
                      jnp.where(scores[1] == gmax, 1.0, jnp.where(scores[2] == gmax, 2.0, 3.0)))
    in_grp = jnp.floor(eidx * (1.0 / EXPERTS_PER_GROUP)) == gstar
    cand = jnp.where(in_grp, sel, -jnp.inf)
    m1 = jnp.max(cand, axis=0, keepdims=True)
    idx1 = jnp.min(jnp.where(cand == m1, eidx, float(N_EXPERTS)), axis=0, keepdims=True)
    cand2 = jnp.where(eidx == idx1, -jnp.inf, cand)
    m2 = jnp.max(cand2, axis=0, keepdims=True)
    idx2 = jnp.min(jnp.where(cand2 == m2, eidx, float(N_EXPERTS)), axis=0, keepdims=True)
    a1 = jnp.sum(jnp.where(eidx == idx1, aff, 0.0), axis=0, keepdims=True)
    a2 = jnp.sum(jnp.where(eidx == idx2, aff, 0.0), axis=0, keepdims=True)
    den = a1 + a2
    return jnp.where(eidx == idx1, a1 / den, 0.0) + jnp.where(eidx == idx2, a2 / den, 0.0)


def _post_kernel(x_ref, aT_ref, p_ref, pprev_ref, pnext_ref, mod_ref, wot_ref, wob_ref, pw_ref,
                 ps_ref, g2_ref, wr_ref, rb_ref, xmid_ref, h2_ref, gT_ref, *, seq_len):
    tm = x_ref.shape[1]
    i = pl.program_id(1)
    n_tiles = seq_len // tm
    m = mod_ref[0]
    gate_msa = m[:, 2 * D_MODEL:3 * D_MODEL]
    shift2 = m[:, 3 * D_MODEL:4 * D_MODEL]
    scale2 = m[:, 4 * D_MODEL:5 * D_MODEL]

    pc = p_ref[0].astype(F32)
    prev = jnp.where(i > 0, pprev_ref[0].astype(F32), 0.0)
    nxt = jnp.where(i < n_tiles - 1, pnext_ref[0].astype(F32), 0.0)
    ext = jnp.concatenate([prev, pc, nxt], axis=0)
    t_seq = lax.broadcasted_iota(jnp.int32, (tm, POOL_GROUP_WIDTH), 0) + i * tm
    pools = []
    for g, w in enumerate(POOL_WINDOWS):
        lanes = slice(g * POOL_GROUP_WIDTH, (g + 1) * POOL_GROUP_WIDTH)
        a = ext[:, lanes]
        span = 1
        while span < w:
            a = a[:-span] + a[span:]
            span *= 2
        start = POOL_HALO - w // 2
        wsum = a[start:start + tm]
        lo = jnp.maximum(t_seq - w // 2, 0)
        hi = jnp.minimum(t_seq + w // 2, seq_len)
        pooled = wsum / (hi - lo).astype(F32) - pc[:, lanes]
        mixed = jnp.dot(pooled.astype(BF16), pw_ref[g], preferred_element_type=F32)
        pools.append(mixed * ps_ref[:, lanes])
    pool = jnp.concatenate(pools, axis=1).astype(BF16)

    proj = lax.dot_general(aT_ref[0], wot_ref[...], TN_DIMS, preferred_element_type=F32)
    proj = proj + jnp.dot(pool, wob_ref[...], preferred_element_type=F32)
    xm = x_ref[0] + gate_msa * proj
    xmid_ref[0] = xm

    h2 = _rms_modulate(xm, g2_ref[...], shift2, scale2)
    h_hi = h2.astype(BF16)
    h2_ref[0] = h_hi
    h_lo = (h2 - h_hi.astype(F32)).astype(BF16)
    r = lax.dot_general(wr_ref[...], jnp.concatenate([h_hi, h_lo], axis=0), NT_DIMS,
                        preferred_element_type=F32)
    logits = r[0:N_EXPERTS, 0:tm] + r[0:N_EXPERTS, tm:2 * tm] + r[N_EXPERTS:2 * N_EXPERTS, 0:tm]
    gT_ref[0] = _route(logits, rb_ref[...])


def _post(x, aT, p, mod_l, row0, per_batch_mod, wo_top, wo_bot, pool_w, pool_scale, g2, wr2, rb, tm):
    b, t, _ = x.shape
    hb = tm // POOL_HALO
    last = t // POOL_HALO - 1
    mod_map = (lambda bi, i: (row0 + bi, 0, 0)) if per_batch_mod else (lambda bi, i: (row0, 0, 0))
    const2 = lambda bi, i: (0, 0)
    return pl.pallas_call(
        functools.partial(_post_kernel, seq_len=t),
        grid=(b, t // tm),
        in_specs=[
            pl.BlockSpec((1, tm, D_MODEL), lambda bi, i: (bi, i, 0)),
            pl.BlockSpec((1, ATTN_WIDTH, tm), lambda bi, i: (bi, 0, i)),
            pl.BlockSpec((1, tm, POOL_WIDTH), lambda bi, i: (bi, i, 0)),
            pl.BlockSpec((1, POOL_HALO, POOL_WIDTH), lambda bi, i: (bi, jnp.maximum(i * hb - 1, 0), 0)),
            pl.BlockSpec((1, POOL_HALO, POOL_WIDTH), lambda bi, i: (bi, jnp.minimum((i + 1) * hb, last), 0)),
            pl.BlockSpec((1, 1, 6 * D_MODEL), mod_map),
            pl.BlockSpec((ATTN_WIDTH, D_MODEL), const2),
            pl.BlockSpec((POOL_WIDTH, D_MODEL), const2),
            pl.BlockSpec((len(POOL_WINDOWS), POOL_GROUP_WIDTH, POOL_GROUP_WIDTH), lambda bi, i: (0, 0, 0)),
            pl.BlockSpec((1, POOL_WIDTH), const2),
            pl.BlockSpec((1, D_MODEL), const2),
            pl.BlockSpec((2 * N_EXPERTS, D_MODEL), const2),
            pl.BlockSpec((N_EXPERTS, 1), const2),
        ],
        out_specs=[
            pl.BlockSpec((1, tm, D_MODEL), lambda bi, i: (bi, i, 0)),
            pl.BlockSpec((1, tm, D_MODEL), lambda bi, i: (bi, i, 0)),
            pl.BlockSpec((1, N_EXPERTS, tm), lambda bi, i: (bi, 0, i)),
        ],
        out_shape=[
            jax.ShapeDtypeStruct((b, t, D_MODEL), F32),
            jax.ShapeDtypeStruct((b, t, D_MODEL), BF16),
            jax.ShapeDtypeStruct((b, N_EXPERTS, t), F32),
        ],
        compiler_params=_params(("parallel", "parallel")),
        name="post_attention",
    )(x, aT, p, p, p, mod_l, wo_top, wo_bot, pool_w, pool_scale, g2, wr2, rb)


def _moe_kernel(h_ref, xmid_ref, gT_ref, mod_ref, wg_ref, wu_ref, wd_ref, o_ref):
    e = pl.program_id(2)
    tm = h_ref.shape[1]

    @pl.when(e == 0)
    def _():
        o_ref[0] = xmid_ref[0]

    h = h_ref[0]
    a = jnp.dot(h, wg_ref[0], preferred_element_type=F32)
    u = jnp.dot(h, wu_ref[0], preferred_element_type=F32)
    hid = (a * jax.nn.sigmoid(a)) * u
    y = jnp.dot(hid.astype(BF16), wd_ref[0], preferred_element_type=F32)
    grow = gT_ref[0, pl.ds(e, 1), :]
    gcol = jnp.broadcast_to(grow, (V7X_LANES, tm)).T
    gate_mlp = mod_ref[0][:, 5 * D_MODEL:6 * D_MODEL]
    gfull = jnp.concatenate([gcol] * (D_MODEL // V7X_LANES), axis=1)
    o_ref[0] += gate_mlp * (gfull * y)


def _moe(h2, xmid, gT, mod_l, row0, per_batch_mod, wg, wu, wd, tm):
    b, t, _ = h2.shape
    mod_map = (lambda bi, i, e: (row0 + bi, 0, 0)) if per_batch_mod else (lambda bi, i, e: (row0, 0, 0))
    return pl.pallas_call(
        _moe_kernel,
        grid=(b, t // tm, N_EXPERTS),
        in_specs=[
            pl.BlockSpec((1, tm, D_MODEL), lambda bi, i, e: (bi, i, 0)),
            pl.BlockSpec((1, tm, D_MODEL), lambda bi, i, e: (bi, i, 0)),
            pl.BlockSpec((1, N_EXPERTS, tm), lambda bi, i, e: (bi, 0, i)),
            pl.BlockSpec((1, 1, 6 * D_MODEL), mod_map),
            pl.BlockSpec((1, D_MODEL, D_EXPERT), lambda bi, i, e: (e, 0, 0)),
            pl.BlockSpec((1, D_MODEL, D_EXPERT), lambda bi, i, e: (e, 0, 0)),
            pl.BlockSpec((1, D_EXPERT, D_MODEL), lambda bi, i, e: (e, 0, 0)),
        ],
        out_specs=pl.BlockSpec((1, tm, D_MODEL), lambda bi, i, e: (bi, i, 0)),
        out_shape=jax.ShapeDtypeStruct((b, t, D_MODEL), F32),
        compiler_params=_params(("parallel", "parallel", "arbitrary")),
        name="experts",
    )(h2, xmid, gT, mod_l, wg, wu, wd)


def _rope_tables(t):
    half = HEAD_DIM // 2
    inv_freq = ROPE_THETA ** (-jnp.arange(0, half, 2, dtype=F32) / half)
    pos = jnp.arange(t, dtype=jnp.int32)
    ang_r = (pos // GRID_W).astype(F32)[None, :] * inv_freq[:, None]
    ang_c = (pos % GRID_W).astype(F32)[None, :] * inv_freq[:, None]
    return jnp.cos(ang_r), jnp.sin(ang_r), jnp.cos(ang_c), jnp.sin(ang_c)


def kernel(x_prompt, x_sample, cache_k, cache_v, c, c_ctx, norm1_g, norm2_g, ada_w, ada_b, w_in,
           q_norm_g, k_norm_g, pool_w, pool_scale, w_out, router_w, router_bias, expert_w_gate,
           expert_w_up, expert_w_down):
    n_ctx, t_ctx, _ = x_prompt.shape
    n_lat, t_lat, _ = x_sample.shape
    past = cache_k.shape[2]

    cond = jnp.concatenate([c_ctx[None, :], c,
                            jnp.zeros((MOD_ROWS - 1 - n_lat, D_MODEL), F32)], axis=0)
    mod = _modulation(cond, ada_w, ada_b).reshape(DEPTH, MOD_ROWS, 1, 6 * D_MODEL)
    rope_tabs = _rope_tables(t_lat)

    wrT = router_w.T
    wr_hi = wrT.astype(BF16)
    wr2 = jnp.concatenate([wr_hi, (wrT - wr_hi.astype(F32)).astype(BF16)], axis=0)
    rb = router_bias.reshape(N_EXPERTS, 1)

    xp, xs = x_prompt, x_sample
    new_k, new_v = [], []
    for l in range(DEPTH):
        w_in_b = w_in[l].astype(BF16)
        wqkvT = w_in_b[:, :QKV_WIDTH].T
        wp = w_in_b[:, QKV_WIDTH:]
        wo = w_out[l].astype(BF16)
        wo_top, wo_bot = wo[:ATTN_WIDTH], wo[ATTN_WIDTH:]
        pw = pool_w[l].astype(BF16)
        ps = pool_scale[l].reshape(1, POOL_WIDTH)
        g1 = norm1_g[l].reshape(1, D_MODEL)
        g2 = norm2_g[l].reshape(1, D_MODEL)
        gq = q_norm_g[l].reshape(HEAD_DIM, 1)
        gk = k_norm_g[l].reshape(HEAD_DIM, 1)
        wg = expert_w_gate[l].astype(BF16)
        wu = expert_w_up[l].astype(BF16)
        wd = expert_w_down[l].astype(BF16)
        mod_l = mod[l]

        qT, k, vT, p, k32, v32 = _pre(xp, mod_l, 0, g1, wqkvT, wp, gq, gk, None, t_ctx, True)
        aT = _attention(qT, k, vT, t_ctx, t_ctx, 1)
        xmid, h2, gT = _post(xp, aT, p, mod_l, 0, False, wo_top, wo_bot, pw, ps, g2, wr2, rb, t_ctx)
        xp = _moe(h2, xmid, gT, mod_l, 0, False, wg, wu, wd, t_ctx)
        new_k.append(k32.reshape(n_ctx, t_ctx, N_KV_HEADS, HEAD_DIM))
        new_v.append(v32.reshape(n_ctx, t_ctx, N_KV_HEADS, HEAD_DIM))

        qT, k, vT, p = _pre(xs, mod_l, 1, g1, wqkvT, wp, gq, gk, rope_tabs, 512, False)
        ck = cache_k[:, l].reshape(n_lat, past, KV_WIDTH).astype(BF16)
        cvT = cache_v[:, l].reshape(n_lat, past, KV_WIDTH).astype(BF16).transpose(0, 2, 1)
        k_all = jnp.concatenate([ck, k], axis=1)
        vT_all = jnp.concatenate([cvT, vT], axis=2)
        aT = _attention(qT, k_all, vT_all, 256, 512, 3)
        xmid, h2, gT = _post(xs, aT, p, mod_l, 1, True, wo_top, wo_bot, pw, ps, g2, wr2, rb, 256)
        xs = _moe(h2, xmid, gT, mod_l, 1, True, wg, wu, wd, 512)

    return xp, xs, jnp.stack(new_k, axis=1), jnp.stack(new_v, axis=1)
```

```python
import functools

import jax
import jax.numpy as jnp
from jax import lax
from jax.experimental import pallas as pl
from jax.experimental.pallas import tpu as pltpu

D_MODEL = 1024
DEPTH = 4
GRID_W = 64
HEAD_DIM = 64
N_Q_HEADS = 8
N_KV_HEADS = 2
Q_PER_KV = N_Q_HEADS // N_KV_HEADS
ATTN_WIDTH = N_Q_HEADS * HEAD_DIM
KV_WIDTH = N_KV_HEADS * HEAD_DIM
QKV_WIDTH = ATTN_WIDTH + 2 * KV_WIDTH
POOL_WINDOWS = (2, 4, 8, 16)
POOL_WIDTH = D_MODEL - ATTN_WIDTH
POOL_GROUP_WIDTH = POOL_WIDTH // len(POOL_WINDOWS)
N_EXPERTS = 16
N_EXPERT_GROUPS = 4
EXPERTS_PER_GROUP = N_EXPERTS // N_EXPERT_GROUPS
D_EXPERT = 512
ROPE_THETA = 10000.0
EPS = 1e-6

Q_SCALE = HEAD_DIM ** -0.5 * 1.4426950408889634
DENOM_ROWS = 16
MOD_ROWS = 16
POOL_HALO = 16
V7X_LANES = 128
VMEM_LIMIT = 48 * 1024 * 1024

PAIRS = tuple((i, j) for i in range(EXPERTS_PER_GROUP) for j in range(i + 1, EXPERTS_PER_GROUP))
N_CLASSES = N_EXPERT_GROUPS * len(PAIRS)
ROUTE_ROWS = 8
MOE_TILE = 256

F32 = jnp.float32
BF16 = jnp.bfloat16
NT_DIMS = (((1,), (1,)), ((), ()))
TN_DIMS = (((0,), (0,)), ((), ()))


def _params(sem):
    return pltpu.CompilerParams(dimension_semantics=sem, vmem_limit_bytes=VMEM_LIMIT)


def _mod_kernel(c_ref, w_ref, b_ref, o_ref):
    c = c_ref[...]
    a = c * jax.nn.sigmoid(c)
    o_ref[0] = jnp.dot(a, w_ref[0], preferred_element_type=F32,
                       precision=lax.Precision.HIGHEST) + b_ref[0]


def _modulation(cond, ada_w, ada_b):
    tn = 1536
    n_out = ada_w.shape[2]
    return pl.pallas_call(
        _mod_kernel,
        grid=(DEPTH, n_out // tn),
        in_specs=[
            pl.BlockSpec((MOD_ROWS, D_MODEL), lambda l, n: (0, 0)),
            pl.BlockSpec((1, D_MODEL, tn), lambda l, n: (l, 0, n)),
            pl.BlockSpec((1, 1, tn), lambda l, n: (l, 0, n)),
        ],
        out_specs=pl.BlockSpec((1, MOD_ROWS, tn), lambda l, n: (l, 0, n)),
        out_shape=jax.ShapeDtypeStruct((DEPTH, MOD_ROWS, n_out), F32),
        compiler_params=_params(("arbitrary", "arbitrary")),
        name="modulation",
    )(cond, ada_w, ada_b.reshape(DEPTH, 1, n_out))


def _rms_modulate(x, g, shift, scale):
    ms = jnp.mean(x * x, axis=-1, keepdims=True)
    h = x * lax.rsqrt(ms + EPS) * g
    return h * (1.0 + scale) + shift


def _pre_kernel(*refs, rope, emit_f32, residual):
    refs = list(refs)
    x_ref = refs.pop(0)
    if residual:
        y_ref, modp_ref = refs.pop(0), refs.pop(0)
    mod_ref, g1_ref, wqkv_ref, wp_ref, gq_ref, gk_ref = refs[:6]
    refs = refs[6:]
    if rope:
        cr_ref, sr_ref, cc_ref, sc_ref = refs[:4]
        refs = refs[4:]
    qT_ref, k_ref, vT_ref, p_ref = refs[:4]
    refs = refs[4:]
    tm = x_ref.shape[1]
    x = x_ref[0]
    if residual:
        x = x + modp_ref[0][:, 5 * D_MODEL:6 * D_MODEL] * y_ref[...]
        refs.pop(0)[0] = x
    m = mod_ref[0]
    h = _rms_modulate(x, g1_ref[...], m[:, 0:D_MODEL], m[:, D_MODEL:2 * D_MODEL])
    hb = h.astype(BF16)
    zT = lax.dot_general(wqkv_ref[...], hb, NT_DIMS, preferred_element_type=F32)
    p_ref[0] = jnp.dot(hb, wp_ref[...], preferred_element_type=F32).astype(BF16)

    def norm_rope(zt, n_heads, g_ref):
        z3 = zt.reshape(n_heads, HEAD_DIM, tm)
        ms = jnp.mean(z3 * z3, axis=1, keepdims=True)
        y = z3 * lax.rsqrt(ms + EPS) * g_ref[...][None]
        if rope:
            cr, sr = cr_ref[...][None], sr_ref[...][None]
            cc, sc = cc_ref[...][None], sc_ref[...][None]
            x1r, x2r, x1c, x2c = y[:, 0:16], y[:, 16:32], y[:, 32:48], y[:, 48:64]
            y = jnp.concatenate([x1r * cr - x2r * sr, x1r * sr + x2r * cr,
                                 x1c * cc - x2c * sc, x1c * sc + x2c * cc], axis=1)
        return y.reshape(n_heads * HEAD_DIM, tm)

    qT = norm_rope(zT[0:ATTN_WIDTH], N_Q_HEADS, gq_ref)
    qT_ref[0] = (qT * Q_SCALE).astype(BF16)
    kT = norm_rope(zT[ATTN_WIDTH:ATTN_WIDTH + KV_WIDTH], N_KV_HEADS, gk_ref)
    k_tok = kT.T
    k_ref[0] = k_tok.astype(BF16)
    vT = zT[ATTN_WIDTH + KV_WIDTH:QKV_WIDTH]
    vT_ref[0] = vT.astype(BF16)
    if emit_f32:
        k32_ref, v32_ref = refs[:2]
        k32_ref[0] = k_tok
        v32_ref[0] = vT.T


def _pre(x, res, mod_l, row0, per_batch_mod, g1, wqkvT, wp, gq, gk, rope_tabs, tm, emit_f32):
    b, t, _ = x.shape
    rope = rope_tabs is not None
    mod_map = (lambda bi, i: (row0 + bi, 0, 0)) if per_batch_mod else (lambda bi, i: (row0, 0, 0))
    tok_spec = pl.BlockSpec((1, tm, D_MODEL), lambda bi, i: (bi, i, 0))
    in_specs, args = [tok_spec], [x]
    if res is not None:
        y_all, row_off, mod_prev = res
        blk0, per_b = row_off // tm, t // tm
        in_specs += [pl.BlockSpec((tm, D_MODEL), lambda bi, i: (blk0 + bi * per_b + i, 0)),
                     pl.BlockSpec((1, 1, 6 * D_MODEL), mod_map)]
        args += [y_all, mod_prev]
    in_specs += [
        pl.BlockSpec((1, 1, 6 * D_MODEL), mod_map),
        pl.BlockSpec((1, D_MODEL), lambda bi, i: (0, 0)),
        pl.BlockSpec((QKV_WIDTH, D_MODEL), lambda bi, i: (0, 0)),
        pl.BlockSpec((D_MODEL, POOL_WIDTH), lambda bi, i: (0, 0)),
        pl.BlockSpec((HEAD_DIM, 1), lambda bi, i: (0, 0)),
        pl.BlockSpec((HEAD_DIM, 1), lambda bi, i: (0, 0)),
    ]
    args += [mod_l, g1, wqkvT, wp, gq, gk]
    if rope:
        in_specs += [pl.BlockSpec((16, tm), lambda bi, i: (0, i))] * 4
        args += list(rope_tabs)
    out_shape = [
        jax.ShapeDtypeStruct((b, ATTN_WIDTH, t), BF16),
        jax.ShapeDtypeStruct((b, t, KV_WIDTH), BF16),
        jax.ShapeDtypeStruct((b, KV_WIDTH, t), BF16),
        jax.ShapeDtypeStruct((b, t, POOL_WIDTH), BF16),
    ]
    out_specs = [
        pl.BlockSpec((1, ATTN_WIDTH, tm), lambda bi, i: (bi, 0, i)),
        pl.BlockSpec((1, tm, KV_WIDTH), lambda bi, i: (bi, i, 0)),
        pl.BlockSpec((1, KV_WIDTH, tm), lambda bi, i: (bi, 0, i)),
        pl.BlockSpec((1, tm, POOL_WIDTH), lambda bi, i: (bi, i, 0)),
    ]
    if res is not None:
        out_shape.append(jax.ShapeDtypeStruct((b, t, D_MODEL), F32))
        out_specs.append(tok_spec)
    if emit_f32:
        out_shape += [jax.ShapeDtypeStruct((b, t, KV_WIDTH), F32)] * 2
        out_specs += [pl.BlockSpec((1, tm, KV_WIDTH), lambda bi, i: (bi, i, 0))] * 2
    return pl.pallas_call(
        functools.partial(_pre_kernel, rope=rope, emit_f32=emit_f32, residual=res is not None),
        grid=(b, t // tm),
        in_specs=in_specs,
        out_specs=out_specs,
        out_shape=out_shape,
        compiler_params=_params(("parallel", "parallel")),
        name="pre_attention",
    )(*args)


def _finish_kernel(x_ref, y_ref, mod_ref, o_ref):
    o_ref[0] = x_ref[0] + mod_ref[0][:, 5 * D_MODEL:6 * D_MODEL] * y_ref[...]


def _finish(xmid, y_all, row_off, mod_l, row0, per_batch_mod, tm):
    b, t, _ = xmid.shape
    blk0, per_b = row_off // tm, t // tm
    mod_map = (lambda bi, i: (row0 + bi, 0, 0)) if per_batch_mod else (lambda bi, i: (row0, 0, 0))
    tok_spec = pl.BlockSpec((1, tm, D_MODEL), lambda bi, i: (bi, i, 0))
    return pl.pallas_call(
        _finish_kernel,
        grid=(b, t // tm),
        in_specs=[tok_spec,
                  pl.BlockSpec((tm, D_MODEL), lambda bi, i: (blk0 + bi * per_b + i, 0)),
                  pl.BlockSpec((1, 1, 6 * D_MODEL), mod_map)],
        out_specs=tok_spec,
        out_shape=jax.ShapeDtypeStruct((b, t, D_MODEL), F32),
        compiler_params=_params(("parallel", "parallel")),
        name="moe_residual",
    )(xmid, y_all, mod_l)


def _attn_kernel(qT_ref, k_ref, vT_ref, o_ref, qpad_ref, acc_ref, m_ref, *, tk, unroll):
    tq = qT_ref.shape[2]
    s_len = k_ref.shape[1]
    n_pairs = N_Q_HEADS // 2
    zeros = jnp.zeros((HEAD_DIM, tq), BF16)
    for h in range(N_Q_HEADS):
        qh = qT_ref[0, h * HEAD_DIM:(h + 1) * HEAD_DIM, :]
        qpad_ref[h // 2, :, (h % 2) * tq:(h % 2 + 1) * tq] = jnp.concatenate(
            [qh, zeros] if h < Q_PER_KV else [zeros, qh], axis=0)
    m_ref[...] = jnp.full(m_ref.shape, -jnp.inf, F32)
    acc_ref[...] = jnp.zeros(acc_ref.shape, F32)

    n_items = unroll * n_pairs

    def step(t, carry):
        ones = jnp.ones((DENOM_ROWS, tk), BF16)

        def keys(u):
            off = pl.multiple_of((t * unroll + u) * tk, tk)
            return k_ref[0, pl.ds(off, tk), :]

        def values(u, j):
            off = pl.multiple_of((t * unroll + u) * tk, tk)
            vb = vT_ref[0, j * HEAD_DIM:(j + 1) * HEAD_DIM, pl.ds(off, tk)]
            return jnp.concatenate([vb, ones], axis=0)

        def scores(i):
            return jnp.dot(keys(i // n_pairs), qpad_ref[i % n_pairs],
                           preferred_element_type=F32)

        queue = [scores(0), scores(1)]
        for i in range(n_items):
            u, pr = divmod(i, n_pairs)
            j = (2 * pr) // Q_PER_KV
            s = queue.pop(0)
            m_old = m_ref[pr]
            m_new = jnp.maximum(m_old, jnp.max(s, axis=0, keepdims=True))
            alpha = jnp.exp2(m_old - m_new)
            p = jnp.exp2(s - m_new).astype(BF16)
            if i + 2 < n_items:
                queue.append(scores(i + 2))
            pv = jnp.dot(values(u, j), p, preferred_element_type=F32)
            acc_ref[pr] = alpha * acc_ref[pr] + pv
            m_ref[pr] = m_new
        return carry

    lax.fori_loop(0, s_len // (tk * unroll), step, 0)
    for h in range(N_Q_HEADS):
        lanes = slice((h % 2) * tq, (h % 2 + 1) * tq)
        o_ref[0, h * HEAD_DIM:(h + 1) * HEAD_DIM, :] = (
            acc_ref[h // 2, 0:HEAD_DIM, lanes]
            / acc_ref[h // 2, HEAD_DIM:HEAD_DIM + 1, lanes]).astype(BF16)


def _attention(qT, k_all, vT_all, tq, tk, unroll):
    b, _, t = qT.shape
    s_len = k_all.shape[1]
    return pl.pallas_call(
        functools.partial(_attn_kernel, tk=tk, unroll=unroll),
        grid=(b, t // tq),
        in_specs=[
            pl.BlockSpec((1, ATTN_WIDTH, tq), lambda bi, i: (bi, 0, i)),
            pl.BlockSpec((1, s_len, KV_WIDTH), lambda bi, i: (bi, 0, 0)),
            pl.BlockSpec((1, KV_WIDTH, s_len), lambda bi, i: (bi, 0, 0)),
        ],
        out_specs=pl.BlockSpec((1, ATTN_WIDTH, tq), lambda bi, i: (bi, 0, i)),
        out_shape=jax.ShapeDtypeStruct((b, ATTN_WIDTH, t), BF16),
        scratch_shapes=[
            pltpu.VMEM((N_Q_HEADS // 2, 2 * HEAD_DIM, 2 * tq), BF16),
            pltpu.VMEM((N_Q_HEADS // 2, HEAD_DIM + DENOM_ROWS, 2 * tq), F32),
            pltpu.VMEM((N_Q_HEADS // 2, 1, 2 * tq), F32),
        ],
        compiler_params=_params(("parallel", "parallel")),
        name="attention",
    )(qT, k_all, vT_all)


def _route(logits, bias):
    aff = jax.nn.sigmoid(logits)
    sel = aff + bias
    eidx = lax.broadcasted_iota(jnp.int32, sel.shape, 0).astype(F32)
    scores = []
    for g in range(N_EXPERT_GROUPS):
        r = [sel[EXPERTS_PER_GROUP * g + i:EXPERTS_PER_GROUP * g + i + 1, :]
             for i in range(EXPERTS_PER_GROUP)]
        best = None
        for i, j in PAIRS:
            sij = r[i] + r[j]
            best = sij if best is None else jnp.maximum(best, sij)
        scores.append(best)
    gmax = functools.reduce(jnp.maximum, scores)
    gstar = jnp.where(scores[0] == gmax, 0.0,
                      jnp.where(scores[1] == gmax, 1.0, jnp.where(scores[2] == gmax, 2.0, 3.0)))
    in_grp = jnp.floor(eidx * (1.0 / EXPERTS_PER_GROUP)) == gstar
    cand = jnp.where(in_grp, sel, -jnp.inf)
    m1 = jnp.max(cand, axis=0, keepdims=True)
    idx1 = jnp.min(jnp.where(cand == m1, eidx, float(N_EXPERTS)), axis=0, keepdims=True)
    cand2 = jnp.where(eidx == idx1, -jnp.inf, cand)
    m2 = jnp.max(cand2, axis=0, keepdims=True)
    idx2 = jnp.min(jnp.where(cand2 == m2, eidx, float(N_EXPERTS)), axis=0, keepdims=True)
    lo, hi = jnp.minimum(idx1, idx2), jnp.maximum(idx1, idx2)
    a_lo = jnp.sum(jnp.where(eidx == lo, aff, 0.0), axis=0, keepdims=True)
    a_hi = jnp.sum(jnp.where(eidx == hi, aff, 0.0), axis=0, keepdims=True)
    den = a_lo + a_hi
    i_in = lo - EXPERTS_PER_GROUP * gstar
    j_in = hi - EXPERTS_PER_GROUP * gstar
    first = jnp.where(i_in == 0.0, 0.0, jnp.where(i_in == 1.0, 2.0, 3.0))
    cls = gstar * len(PAIRS) + first + j_in - 1.0
    pad = jnp.zeros((ROUTE_ROWS - 3, logits.shape[1]), F32)
    return jnp.concatenate([cls, a_lo / den, a_hi / den, pad], axis=0)


def _post_kernel(x_ref, aT_ref, p_ref, pprev_ref, pnext_ref, mod_ref, wot_ref, wob_ref, pw_ref,
                 ps_ref, g2_ref, wr_ref, rb_ref, h2_in_ref, xmid_ref, h2_ref, route_ref, *, seq_len):
    del h2_in_ref
    tm = x_ref.shape[1]
    i = pl.program_id(1)
    n_tiles = seq_len // tm
    m = mod_ref[0]
    gate_msa = m[:, 2 * D_MODEL:3 * D_MODEL]
    shift2 = m[:, 3 * D_MODEL:4 * D_MODEL]
    scale2 = m[:, 4 * D_MODEL:5 * D_MODEL]

    pc = p_ref[0].astype(F32)
    prev = jnp.where(i > 0, pprev_ref[0].astype(F32), 0.0)
    nxt = jnp.where(i < n_tiles - 1, pnext_ref[0].astype(F32), 0.0)
    ext = jnp.concatenate([prev, pc, nxt], axis=0)
    t_seq = lax.broadcasted_iota(jnp.int32, (tm, POOL_GROUP_WIDTH), 0) + i * tm
    pools = []
    for g, w in enumerate(POOL_WINDOWS):
        lanes = slice(g * POOL_GROUP_WIDTH, (g + 1) * POOL_GROUP_WIDTH)
        a = ext[:, lanes]
        span = 1
        while span < w:
            a = a[:-span] + a[span:]
            span *= 2
        start = POOL_HALO - w // 2
        wsum = a[start:start + tm]
        lo = jnp.maximum(t_seq - w // 2, 0)
        hi = jnp.minimum(t_seq + w // 2, seq_len)
        pooled = wsum / (hi - lo).astype(F32) - pc[:, lanes]
        mixed = jnp.dot(pooled.astype(BF16), pw_ref[g], preferred_element_type=F32)
        pools.append(mixed * ps_ref[:, lanes])
    pool = jnp.concatenate(pools, axis=1).astype(BF16)

    proj = lax.dot_general(aT_ref[0], wot_ref[...], TN_DIMS, preferred_element_type=F32)
    proj = proj + jnp.dot(pool, wob_ref[...], preferred_element_type=F32)
    xm = x_ref[0] + gate_msa * proj
    xmid_ref[0] = xm

    h2 = _rms_modulate(xm, g2_ref[...], shift2, scale2)
    h2_ref[...] = h2
    h_hi = h2.astype(BF16)
    h_lo = (h2 - h_hi.astype(F32)).astype(BF16)
    r = lax.dot_general(wr_ref[...], jnp.concatenate([h_hi, h_lo], axis=0), NT_DIMS,
                        preferred_element_type=F32)
    logits = r[0:N_EXPERTS, 0:tm] + r[0:N_EXPERTS, tm:2 * tm] + r[N_EXPERTS:2 * N_EXPERTS, 0:tm]
    route_ref[0] = _route(logits, rb_ref[...])


def _post(x, aT, p, h2_buf, row_off, mod_l, row0, per_batch_mod, wo_top, wo_bot, pool_w,
          pool_scale, g2, wr2, rb, tm):
    b, t, _ = x.shape
    n_total = h2_buf.shape[0]
    hb = tm // POOL_HALO
    last = t // POOL_HALO - 1
    blk0, per_b = row_off // tm, t // tm
    mod_map = (lambda bi, i: (row0 + bi, 0, 0)) if per_batch_mod else (lambda bi, i: (row0, 0, 0))
    const2 = lambda bi, i: (0, 0)
    tok_spec = pl.BlockSpec((1, tm, D_MODEL), lambda bi, i: (bi, i, 0))
    in_specs = [
        tok_spec,
        pl.BlockSpec((1, ATTN_WIDTH, tm), lambda bi, i: (bi, 0, i)),
        pl.BlockSpec((1, tm, POOL_WIDTH), lambda bi, i: (bi, i, 0)),
        pl.BlockSpec((1, POOL_HALO, POOL_WIDTH), lambda bi, i: (bi, jnp.maximum(i * hb - 1, 0), 0)),
        pl.BlockSpec((1, POOL_HALO, POOL_WIDTH), lambda bi, i: (bi, jnp.minimum((i + 1) * hb, last), 0)),
        pl.BlockSpec((1, 1, 6 * D_MODEL), mod_map),
        pl.BlockSpec((ATTN_WIDTH, D_MODEL), const2),
        pl.BlockSpec((POOL_WIDTH, D_MODEL), const2),
        pl.BlockSpec((len(POOL_WINDOWS), POOL_GROUP_WIDTH, POOL_GROUP_WIDTH), lambda bi, i: (0, 0, 0)),
        pl.BlockSpec((1, POOL_WIDTH), const2),
        pl.BlockSpec((1, D_MODEL), const2),
        pl.BlockSpec((2 * N_EXPERTS, D_MODEL), const2),
        pl.BlockSpec((N_EXPERTS, 1), const2),
        pl.BlockSpec(memory_space=pl.ANY),
    ]
    args = [x, aT, p, p, p, mod_l, wo_top, wo_bot, pool_w, pool_scale, g2, wr2, rb, h2_buf]
    aliases = {len(args) - 1: 1}
    return pl.pallas_call(
        functools.partial(_post_kernel, seq_len=t),
        grid=(b, t // tm),
        in_specs=in_specs,
        out_specs=[
            tok_spec,
            pl.BlockSpec((tm, D_MODEL), lambda bi, i: (blk0 + bi * per_b + i, 0)),
            pl.BlockSpec((1, ROUTE_ROWS, tm), lambda bi, i: (bi, 0, i)),
        ],
        out_shape=[
            jax.ShapeDtypeStruct((b, t, D_MODEL), F32),
            jax.ShapeDtypeStruct((n_total, D_MODEL), F32),
            jax.ShapeDtypeStruct((b, ROUTE_ROWS, t), F32),
        ],
        input_output_aliases=aliases,
        compiler_params=_params(("arbitrary", "arbitrary")),
        name="post_attention",
    )(*args)


def _dispatch(route_rows, n_tokens, n_tiles):
    tile = MOE_TILE
    cls = route_rows[:, 0].astype(jnp.int32)
    order = jnp.argsort(cls, stable=True).astype(jnp.int32)
    sorted_cls = cls[order]
    classes = jnp.arange(N_CLASSES, dtype=jnp.int32)
    starts = jnp.searchsorted(sorted_cls, classes, side="left").astype(jnp.int32)
    counts = jnp.diff(jnp.concatenate([starts, jnp.array([n_tokens], jnp.int32)]))
    tiles_per = (counts + tile - 1) // tile
    tile_end = jnp.cumsum(tiles_per)
    tile_start = tile_end - tiles_per
    n_used = tile_end[-1]
    t_ids = jnp.arange(n_tiles, dtype=jnp.int32)
    t_eff = jnp.minimum(t_ids, n_used - 1)
    tile_cls = jnp.searchsorted(tile_end, t_eff, side="right").astype(jnp.int32)
    grp, pair = tile_cls // len(PAIRS), tile_cls % len(PAIRS)
    pair_i = jnp.array([i for i, _ in PAIRS], jnp.int32)[pair]
    pair_j = jnp.array([j for _, j in PAIRS], jnp.int32)[pair]
    ea = grp * EXPERTS_PER_GROUP + pair_i
    eb = grp * EXPERTS_PER_GROUP + pair_j
    r_ids = jnp.arange(tile, dtype=jnp.int32)[None, :]
    within = (t_ids - tile_start[tile_cls])[:, None] * tile + r_ids
    valid = (within < counts[tile_cls][:, None]) & (t_ids < n_used)[:, None]
    k = jnp.clip(starts[tile_cls][:, None] + within, 0, n_tokens - 1)
    src = jnp.where(valid, order[k], 0)
    dst = jnp.where(valid, src, n_tokens + (t_ids % 2)[:, None] * tile + r_ids)
    w_lo = jnp.where(valid, route_rows[:, 1][src], 0.0)
    w_hi = jnp.where(valid, route_rows[:, 2][src], 0.0)
    w = jnp.stack([w_lo, w_hi], axis=1)
    return (ea, eb, n_used.reshape(1).astype(jnp.int32),
            src.reshape(n_tiles, 1, tile), dst.reshape(n_tiles, 1, tile), w)


def _experts_kernel(ea_ref, eb_ref, nu_ref, src_ref, srcn_ref, dst_ref, w_ref, h_hbm,
                    wga_ref, wua_ref, wda_ref, wgb_ref, wub_ref, wdb_ref, y_in_hbm, y_hbm,
                    xbuf, ybuf, gsem, ssem):
    del y_in_hbm
    t = pl.program_id(0)
    n_t = pl.num_programs(0)
    n_used = nu_ref[0]
    tile = xbuf.shape[1]
    slot = t % 2

    def start_gather(idx_ref, s):
        for r in range(tile):
            pltpu.make_async_copy(h_hbm.at[pl.ds(idx_ref[0, 0, r], 1), :],
                                  xbuf.at[s, pl.ds(r, 1), :], gsem.at[s]).start()

    def wait_gather(s):
        pltpu.make_async_copy(h_hbm.at[pl.ds(0, tile), :], xbuf.at[s], gsem.at[s]).wait()

    def start_scatter(s):
        for r in range(tile):
            pltpu.make_async_copy(ybuf.at[s, pl.ds(r, 1), :],
                                  y_hbm.at[pl.ds(dst_ref[0, 0, r], 1), :], ssem.at[s]).start()

    def wait_scatter(s):
        pltpu.make_async_copy(ybuf.at[s], y_hbm.at[pl.ds(0, tile), :], ssem.at[s]).wait()

    @pl.when(t == 0)
    def _():
        start_gather(src_ref, 0)

    @pl.when(jnp.logical_and(t >= 2, t - 2 < n_used))
    def _():
        wait_scatter(slot)

    @pl.when(t < n_used)
    def _():
        wait_gather(slot)

        @pl.when(t + 1 < n_used)
        def _():
            start_gather(srcn_ref, 1 - slot)

        x = xbuf[slot].astype(BF16)
        wrow = w_ref[0]

        def column(row):
            c = jnp.broadcast_to(row, (V7X_LANES, tile)).T
            return jnp.concatenate([c] * (D_EXPERT // V7X_LANES), axis=1)

        def hidden(wg_ref, wu_ref, gate):
            a = jnp.dot(x, wg_ref[0], preferred_element_type=F32)
            u = jnp.dot(x, wu_ref[0], preferred_element_type=F32)
            return ((a * jax.nn.sigmoid(a)) * u * gate).astype(BF16)

        ha = hidden(wga_ref, wua_ref, column(wrow[0:1]))
        hb = hidden(wgb_ref, wub_ref, column(wrow[1:2]))
        y = jnp.dot(ha, wda_ref[0], preferred_element_type=F32)
        y = y + jnp.dot(hb, wdb_ref[0], preferred_element_type=F32)
        ybuf[slot] = y
        start_scatter(slot)

    @pl.when(t == n_t - 1)
    def _():
        @pl.when(t - 1 < n_used)
        def _():
            wait_scatter(1 - slot)

        @pl.when(t < n_used)
        def _():
            wait_scatter(slot)


def _experts(h2_all, y_buf, disp, wg, wu, wd):
    ea, eb, n_used, src, dst, w = disp
    n_tokens = h2_all.shape[0]
    n_tiles = src.shape[0]
    tile = MOE_TILE
    idx_spec = lambda fn: pl.BlockSpec((1, 1, tile), fn, memory_space=pltpu.SMEM)
    wspec = lambda shape, sel: pl.BlockSpec((1,) + shape, sel)
    grid_spec = pltpu.PrefetchScalarGridSpec(
        num_scalar_prefetch=3,
        grid=(n_tiles,),
        in_specs=[
            idx_spec(lambda t, ea, eb, nu: (t, 0, 0)),
            idx_spec(lambda t, ea, eb, nu: (jnp.minimum(t + 1, n_tiles - 1), 0, 0)),
            idx_spec(lambda t, ea, eb, nu: (t, 0, 0)),
            pl.BlockSpec((1, 2, tile), lambda t, ea, eb, nu: (t, 0, 0)),
            pl.BlockSpec(memory_space=pl.ANY),
            wspec((D_MODEL, D_EXPERT), lambda t, ea, eb, nu: (ea[t], 0, 0)),
            wspec((D_MODEL, D_EXPERT), lambda t, ea, eb, nu: (ea[t], 0, 0)),
            wspec((D_EXPERT, D_MODEL), lambda t, ea, eb, nu: (ea[t], 0, 0)),
            wspec((D_MODEL, D_EXPERT), lambda t, ea, eb, nu: (eb[t], 0, 0)),
            wspec((D_MODEL, D_EXPERT), lambda t, ea, eb, nu: (eb[t], 0, 0)),
            wspec((D_EXPERT, D_MODEL), lambda t, ea, eb, nu: (eb[t], 0, 0)),
            pl.BlockSpec(memory_space=pl.ANY),
        ],
        out_specs=pl.BlockSpec(memory_space=pl.ANY),
        scratch_shapes=[
            pltpu.VMEM((2, tile, D_MODEL), F32),
            pltpu.VMEM((2, tile, D_MODEL), F32),
            pltpu.SemaphoreType.DMA((2,)),
            pltpu.SemaphoreType.DMA((2,)),
        ],
    )
    return pl.pallas_call(
        _experts_kernel,
        grid_spec=grid_spec,
        out_shape=jax.ShapeDtypeStruct((n_tokens + 2 * tile, D_MODEL), F32),
        input_output_aliases={14: 0},
        compiler_params=_params(("arbitrary",)),
        name="experts",
    )(ea, eb, n_used, src, src, dst, w, h2_all, wg, wu, wd, wg, wu, wd, y_buf)


def _rope_tables(t):
    half = HEAD_DIM // 2
    inv_freq = ROPE_THETA ** (-jnp.arange(0, half, 2, dtype=F32) / half)
    pos = jnp.arange(t, dtype=jnp.int32)
    ang_r = (pos // GRID_W).astype(F32)[None, :] * inv_freq[:, None]
    ang_c = (pos % GRID_W).astype(F32)[None, :] * inv_freq[:, None]
    return jnp.cos(ang_r), jnp.sin(ang_r), jnp.cos(ang_c), jnp.sin(ang_c)


def kernel(x_prompt, x_sample, cache_k, cache_v, c, c_ctx, norm1_g, norm2_g, ada_w, ada_b, w_in,
           q_norm_g, k_norm_g, pool_w, pool_scale, w_out, router_w, router_bias, expert_w_gate,
           expert_w_up, expert_w_down):
    n_ctx, t_ctx, _ = x_prompt.shape
    n_lat, t_lat, _ = x_sample.shape
    past = cache_k.shape[2]
    tok_ctx = n_ctx * t_ctx
    n_tokens = tok_ctx + n_lat * t_lat
    n_tiles = n_tokens // MOE_TILE + N_CLASSES
    tm_lat = 512

    cond = jnp.concatenate([c_ctx[None, :], c,
                            jnp.zeros((MOD_ROWS - 1 - n_lat, D_MODEL), F32)], axis=0)
    mod = _modulation(cond, ada_w, ada_b).reshape(DEPTH, MOD_ROWS, 1, 6 * D_MODEL)
    rope_tabs = _rope_tables(t_lat)

    wrT = router_w.T
    wr_hi = wrT.astype(BF16)
    wr2 = jnp.concatenate([wr_hi, (wrT - wr_hi.astype(F32)).astype(BF16)], axis=0)
    rb = router_bias.reshape(N_EXPERTS, 1)

    xp, xs = x_prompt, x_sample
    h2_all = jnp.zeros((n_tokens, D_MODEL), F32)
    y_all = jnp.zeros((n_tokens + 2 * MOE_TILE, D_MODEL), F32)
    new_k, new_v = [], []
    for l in range(DEPTH):
        w_in_b = w_in[l].astype(BF16)
        wqkvT = w_in_b[:, :QKV_WIDTH].T
        wp = w_in_b[:, QKV_WIDTH:]
        wo = w_out[l].astype(BF16)
        wo_top, wo_bot = wo[:ATTN_WIDTH], wo[ATTN_WIDTH:]
        pw = pool_w[l].astype(BF16)
        ps = pool_scale[l].reshape(1, POOL_WIDTH)
        g1 = norm1_g[l].reshape(1, D_MODEL)
        g2 = norm2_g[l].reshape(1, D_MODEL)
        gq = q_norm_g[l].reshape(HEAD_DIM, 1)
        gk = k_norm_g[l].reshape(HEAD_DIM, 1)
        wg = expert_w_gate[l].astype(BF16)
        wu = expert_w_up[l].astype(BF16)
        wd = expert_w_down[l].astype(BF16)
        mod_l = mod[l]
        res_ctx = None if l == 0 else (y_all, 0, mod[l - 1])
        res_lat = None if l == 0 else (y_all, tok_ctx, mod[l - 1])

        outs = _pre(xp, res_ctx, mod_l, 0, False, g1, wqkvT, wp, gq, gk, None, t_ctx, True)
        if l == 0:
            qT, k, vT, p, k32, v32 = outs
        else:
            qT, k, vT, p, xp, k32, v32 = outs
        aT = _attention(qT, k, vT, t_ctx, t_ctx, 1)
        xp, h2_all, route_ctx = _post(xp, aT, p, h2_all, 0, mod_l, 0, False, wo_top, wo_bot,
                                      pw, ps, g2, wr2, rb, t_ctx)
        new_k.append(k32.reshape(n_ctx, t_ctx, N_KV_HEADS, HEAD_DIM))
        new_v.append(v32.reshape(n_ctx, t_ctx, N_KV_HEADS, HEAD_DIM))

        outs = _pre(xs, res_lat, mod_l, 1, True, g1, wqkvT, wp, gq, gk, rope_tabs, tm_lat, False)
        if l == 0:
            qT, k, vT, p = outs
        else:
            qT, k, vT, p, xs = outs
        ck = cache_k[:, l].reshape(n_lat, past, KV_WIDTH).astype(BF16)
        cvT = cache_v[:, l].reshape(n_lat, past, KV_WIDTH).astype(BF16).transpose(0, 2, 1)
        k_all = jnp.concatenate([ck, k], axis=1)
        vT_all = jnp.concatenate([cvT, vT], axis=2)
        aT = _attention(qT, k_all, vT_all, 256, 512, 3)
        xs, h2_all, route_lat = _post(xs, aT, p, h2_all, tok_ctx, mod_l, 1, True, wo_top,
                                      wo_bot, pw, ps, g2, wr2, rb, 256)

        route_rows = jnp.concatenate(
            [route_ctx.transpose(0, 2, 1).reshape(tok_ctx, ROUTE_ROWS),
             route_lat.transpose(0, 2, 1).reshape(n_tokens - tok_ctx, ROUTE_ROWS)], axis=0)
        y_all = _experts(h2_all, y_all, _dispatch(route_rows, n_tokens, n_tiles), wg, wu, wd)

    mod_l = mod[DEPTH - 1]
    xp = _finish(xp, y_all, 0, mod_l, 0, False, t_ctx)
    xs = _finish(xs, y_all, tok_ctx, mod_l, 1, True, tm_lat)
    return xp, xs, jnp.stack(new_k, axis=1), jnp.stack(new_v, axis=1)
```

```python
import functools

import jax
import jax.numpy as jnp
from jax import lax
from jax.experimental import pallas as pl
from jax.experimental.pallas import tpu as pltpu

D_MODEL = 1024
DEPTH = 4
GRID_W = 64
HEAD_DIM = 64
N_Q_HEADS = 8
N_KV_HEADS = 2
Q_PER_KV = N_Q_HEADS // N_KV_HEADS
ATTN_WIDTH = N_Q_HEADS * HEAD_DIM
KV_WIDTH = N_KV_HEADS * HEAD_DIM
QKV_WIDTH = ATTN_WIDTH + 2 * KV_WIDTH
POOL_WINDOWS = (2, 4, 8, 16)
POOL_WIDTH = D_MODEL - ATTN_WIDTH
POOL_GROUP_WIDTH = POOL_WIDTH // len(POOL_WINDOWS)
N_EXPERTS = 16
N_EXPERT_GROUPS = 4
EXPERTS_PER_GROUP = N_EXPERTS // N_EXPERT_GROUPS
D_EXPERT = 512
ROPE_THETA = 10000.0
EPS = 1e-6

Q_SCALE = HEAD_DIM ** -0.5 * 1.4426950408889634
DENOM_ROWS = 16
MOD_ROWS = 16
POOL_HALO = 16
V7X_LANES = 128
VMEM_LIMIT = 48 * 1024 * 1024

PAIRS = tuple((i, j) for i in range(EXPERTS_PER_GROUP) for j in range(i + 1, EXPERTS_PER_GROUP))
N_CLASSES = N_EXPERT_GROUPS * len(PAIRS)
ROUTE_ROWS = 8
MOE_TILE = 256

F32 = jnp.float32
BF16 = jnp.bfloat16
NT_DIMS = (((1,), (1,)), ((), ()))
TN_DIMS = (((0,), (0,)), ((), ()))


def _params(sem):
    return pltpu.CompilerParams(dimension_semantics=sem, vmem_limit_bytes=VMEM_LIMIT)


def _mod_kernel(c_ref, w_ref, b_ref, o_ref):
    c = c_ref[...]
    a = c * jax.nn.sigmoid(c)
    o_ref[0] = jnp.dot(a, w_ref[0], preferred_element_type=F32,
                       precision=lax.Precision.HIGHEST) + b_ref[0]


def _modulation(cond, ada_w, ada_b):
    tn = 1536
    n_out = ada_w.shape[2]
    return pl.pallas_call(
        _mod_kernel,
        grid=(DEPTH, n_out // tn),
        in_specs=[
            pl.BlockSpec((MOD_ROWS, D_MODEL), lambda l, n: (0, 0)),
            pl.BlockSpec((1, D_MODEL, tn), lambda l, n: (l, 0, n)),
            pl.BlockSpec((1, 1, tn), lambda l, n: (l, 0, n)),
        ],
        out_specs=pl.BlockSpec((1, MOD_ROWS, tn), lambda l, n: (l, 0, n)),
        out_shape=jax.ShapeDtypeStruct((DEPTH, MOD_ROWS, n_out), F32),
        compiler_params=_params(("arbitrary", "arbitrary")),
        name="modulation",
    )(cond, ada_w, ada_b.reshape(DEPTH, 1, n_out))


def _rms_modulate(x, g, shift, scale):
    ms = jnp.mean(x * x, axis=-1, keepdims=True)
    h = x * lax.rsqrt(ms + EPS) * g
    return h * (1.0 + scale) + shift


def _pre_kernel(*refs, rope, emit_f32, residual):
    refs = list(refs)
    x_ref = refs.pop(0)
    if residual:
        y_ref, modp_ref = refs.pop(0), refs.pop(0)
    mod_ref, g1_ref, wqkv_ref, wp_ref, gq_ref, gk_ref = refs[:6]
    refs = refs[6:]
    if rope:
        cr_ref, sr_ref, cc_ref, sc_ref = refs[:4]
        refs = refs[4:]
    qT_ref, k_ref, vT_ref, p_ref = refs[:4]
    refs = refs[4:]
    tm = x_ref.shape[1]
    x = x_ref[0]
    if residual:
        x = x + modp_ref[0][:, 5 * D_MODEL:6 * D_MODEL] * y_ref[...]
        refs.pop(0)[0] = x
    m = mod_ref[0]
    h = _rms_modulate(x, g1_ref[...], m[:, 0:D_MODEL], m[:, D_MODEL:2 * D_MODEL])
    hb = h.astype(BF16)
    zT = lax.dot_general(wqkv_ref[...], hb, NT_DIMS, preferred_element_type=F32)
    p_ref[0] = jnp.dot(hb, wp_ref[...], preferred_element_type=F32).astype(BF16)

    def norm_rope(zt, n_heads, g_ref):
        z3 = zt.reshape(n_heads, HEAD_DIM, tm)
        ms = jnp.mean(z3 * z3, axis=1, keepdims=True)
        y = z3 * lax.rsqrt(ms + EPS) * g_ref[...][None]
        if rope:
            cr, sr = cr_ref[...][None], sr_ref[...][None]
            cc, sc = cc_ref[...][None], sc_ref[...][None]
            x1r, x2r, x1c, x2c = y[:, 0:16], y[:, 16:32], y[:, 32:48], y[:, 48:64]
            y = jnp.concatenate([x1r * cr - x2r * sr, x1r * sr + x2r * cr,
                                 x1c * cc - x2c * sc, x1c * sc + x2c * cc], axis=1)
        return y.reshape(n_heads * HEAD_DIM, tm)

    qT = norm_rope(zT[0:ATTN_WIDTH], N_Q_HEADS, gq_ref)
    qT_ref[0] = (qT * Q_SCALE).astype(BF16)
    kT = norm_rope(zT[ATTN_WIDTH:ATTN_WIDTH + KV_WIDTH], N_KV_HEADS, gk_ref)
    k_tok = kT.T
    k_ref[0] = k_tok.astype(BF16)
    vT = zT[ATTN_WIDTH + KV_WIDTH:QKV_WIDTH]
    vT_ref[0] = vT.astype(BF16)
    if emit_f32:
        k32_ref, v32_ref = refs[:2]
        k32_ref[0] = k_tok
        v32_ref[0] = vT.T


def _pre(x, res, mod_l, row0, per_batch_mod, g1, wqkvT, wp, gq, gk, rope_tabs, tm, emit_f32):
    b, t, _ = x.shape
    rope = rope_tabs is not None
    mod_map = (lambda bi, i: (row0 + bi, 0, 0)) if per_batch_mod else (lambda bi, i: (row0, 0, 0))
    tok_spec = pl.BlockSpec((1, tm, D_MODEL), lambda bi, i: (bi, i, 0))
    in_specs, args = [tok_spec], [x]
    if res is not None:
        y_all, row_off, mod_prev = res
        blk0, per_b = row_off // tm, t // tm
        in_specs += [pl.BlockSpec((tm, D_MODEL), lambda bi, i: (blk0 + bi * per_b + i, 0)),
                     pl.BlockSpec((1, 1, 6 * D_MODEL), mod_map)]
        args += [y_all, mod_prev]
    in_specs += [
        pl.BlockSpec((1, 1, 6 * D_MODEL), mod_map),
        pl.BlockSpec((1, D_MODEL), lambda bi, i: (0, 0)),
        pl.BlockSpec((QKV_WIDTH, D_MODEL), lambda bi, i: (0, 0)),
        pl.BlockSpec((D_MODEL, POOL_WIDTH), lambda bi, i: (0, 0)),
        pl.BlockSpec((HEAD_DIM, 1), lambda bi, i: (0, 0)),
        pl.BlockSpec((HEAD_DIM, 1), lambda bi, i: (0, 0)),
    ]
    args += [mod_l, g1, wqkvT, wp, gq, gk]
    if rope:
        in_specs += [pl.BlockSpec((16, tm), lambda bi, i: (0, i))] * 4
        args += list(rope_tabs)
    out_shape = [
        jax.ShapeDtypeStruct((b, ATTN_WIDTH, t), BF16),
        jax.ShapeDtypeStruct((b, t, KV_WIDTH), BF16),
        jax.ShapeDtypeStruct((b, KV_WIDTH, t), BF16),
        jax.ShapeDtypeStruct((b, t, POOL_WIDTH), BF16),
    ]
    out_specs = [
        pl.BlockSpec((1, ATTN_WIDTH, tm), lambda bi, i: (bi, 0, i)),
        pl.BlockSpec((1, tm, KV_WIDTH), lambda bi, i: (bi, i, 0)),
        pl.BlockSpec((1, KV_WIDTH, tm), lambda bi, i: (bi, 0, i)),
        pl.BlockSpec((1, tm, POOL_WIDTH), lambda bi, i: (bi, i, 0)),
    ]
    if res is not None:
        out_shape.append(jax.ShapeDtypeStruct((b, t, D_MODEL), F32))
        out_specs.append(tok_spec)
    if emit_f32:
        out_shape += [jax.ShapeDtypeStruct((b, t, KV_WIDTH), F32)] * 2
        out_specs += [pl.BlockSpec((1, tm, KV_WIDTH), lambda bi, i: (bi, i, 0))] * 2
    return pl.pallas_call(
        functools.partial(_pre_kernel, rope=rope, emit_f32=emit_f32, residual=res is not None),
        grid=(b, t // tm),
        in_specs=in_specs,
        out_specs=out_specs,
        out_shape=out_shape,
        compiler_params=_params(("parallel", "parallel")),
        name="pre_attention",
    )(*args)


def _finish_kernel(x_ref, y_ref, mod_ref, o_ref):
    o_ref[0] = x_ref[0] + mod_ref[0][:, 5 * D_MODEL:6 * D_MODEL] * y_ref[...]


def _finish(xmid, y_all, row_off, mod_l, row0, per_batch_mod, tm):
    b, t, _ = xmid.shape
    blk0, per_b = row_off // tm, t // tm
    mod_map = (lambda bi, i: (row0 + bi, 0, 0)) if per_batch_mod else (lambda bi, i: (row0, 0, 0))
    tok_spec = pl.BlockSpec((1, tm, D_MODEL), lambda bi, i: (bi, i, 0))
    return pl.pallas_call(
        _finish_kernel,
        grid=(b, t // tm),
        in_specs=[tok_spec,
                  pl.BlockSpec((tm, D_MODEL), lambda bi, i: (blk0 + bi * per_b + i, 0)),
                  pl.BlockSpec((1, 1, 6 * D_MODEL), mod_map)],
        out_specs=tok_spec,
        out_shape=jax.ShapeDtypeStruct((b, t, D_MODEL), F32),
        compiler_params=_params(("parallel", "parallel")),
        name="moe_residual",
    )(xmid, y_all, mod_l)


def _attn_kernel(qT_ref, k_ref, vT_ref, o_ref, qpad_ref, acc_ref, m_ref, *, tk, unroll):
    tq = qT_ref.shape[2]
    s_len = k_ref.shape[1]
    n_pairs = N_Q_HEADS // 2
    zeros = jnp.zeros((HEAD_DIM, tq), BF16)
    for h in range(N_Q_HEADS):
        qh = qT_ref[0, h * HEAD_DIM:(h + 1) * HEAD_DIM, :]
        qpad_ref[h // 2, :, (h % 2) * tq:(h % 2 + 1) * tq] = jnp.concatenate(
            [qh, zeros] if h < Q_PER_KV else [zeros, qh], axis=0)
    m_ref[...] = jnp.full(m_ref.shape, -jnp.inf, F32)
    acc_ref[...] = jnp.zeros(acc_ref.shape, F32)

    n_items = unroll * n_pairs

    def step(t, carry):
        ones = jnp.ones((DENOM_ROWS, tk), BF16)

        def keys(u):
            off = pl.multiple_of((t * unroll + u) * tk, tk)
            return k_ref[0, pl.ds(off, tk), :]

        def values(u, j):
            off = pl.multiple_of((t * unroll + u) * tk, tk)
            vb = vT_ref[0, j * HEAD_DIM:(j + 1) * HEAD_DIM, pl.ds(off, tk)]
            return jnp.concatenate([vb, ones], axis=0)

        def scores(i):
            return jnp.dot(keys(i // n_pairs), qpad_ref[i % n_pairs],
                           preferred_element_type=F32)

        queue = [scores(0), scores(1)]
        for i in range(n_items):
            u, pr = divmod(i, n_pairs)
            j = (2 * pr) // Q_PER_KV
            s = queue.pop(0)
            m_old = m_ref[pr]
            m_new = jnp.maximum(m_old, jnp.max(s, axis=0, keepdims=True))
            alpha = jnp.exp2(m_old - m_new)
            p = jnp.exp2(s - m_new).astype(BF16)
            if i + 2 < n_items:
                queue.append(scores(i + 2))
            pv = jnp.dot(values(u, j), p, preferred_element_type=F32)
            acc_ref[pr] = alpha * acc_ref[pr] + pv
            m_ref[pr] = m_new
        return carry

    lax.fori_loop(0, s_len // (tk * unroll), step, 0)
    for h in range(N_Q_HEADS):
        lanes = slice((h % 2) * tq, (h % 2 + 1) * tq)
        o_ref[0, h * HEAD_DIM:(h + 1) * HEAD_DIM, :] = (
            acc_ref[h // 2, 0:HEAD_DIM, lanes]
            / acc_ref[h // 2, HEAD_DIM:HEAD_DIM + 1, lanes]).astype(BF16)


def _attention(qT, k_all, vT_all, tq, tk, unroll):
    b, _, t = qT.shape
    s_len = k_all.shape[1]
    return pl.pallas_call(
        functools.partial(_attn_kernel, tk=tk, unroll=unroll),
        grid=(b, t // tq),
        in_specs=[
            pl.BlockSpec((1, ATTN_WIDTH, tq), lambda bi, i: (bi, 0, i)),
            pl.BlockSpec((1, s_len, KV_WIDTH), lambda bi, i: (bi, 0, 0)),
            pl.BlockSpec((1, KV_WIDTH, s_len), lambda bi, i: (bi, 0, 0)),
        ],
        out_specs=pl.BlockSpec((1, ATTN_WIDTH, tq), lambda bi, i: (bi, 0, i)),
        out_shape=jax.ShapeDtypeStruct((b, ATTN_WIDTH, t), BF16),
        scratch_shapes=[
            pltpu.VMEM((N_Q_HEADS // 2, 2 * HEAD_DIM, 2 * tq), BF16),
            pltpu.VMEM((N_Q_HEADS // 2, HEAD_DIM + DENOM_ROWS, 2 * tq), F32),
            pltpu.VMEM((N_Q_HEADS // 2, 1, 2 * tq), F32),
        ],
        compiler_params=_params(("parallel", "parallel")),
        name="attention",
    )(qT, k_all, vT_all)


def _route(logits, bias):
    aff = jax.nn.sigmoid(logits)
    sel = aff + bias
    eidx = lax.broadcasted_iota(jnp.int32, sel.shape, 0).astype(F32)
    scores = []
    for g in range(N_EXPERT_GROUPS):
        r = [sel[EXPERTS_PER_GROUP * g + i:EXPERTS_PER_GROUP * g + i + 1, :]
             for i in range(EXPERTS_PER_GROUP)]
        best = None
        for i, j in PAIRS:
            sij = r[i] + r[j]
            best = sij if best is None else jnp.maximum(best, sij)
        scores.append(best)
    gmax = functools.reduce(jnp.maximum, scores)
    gstar = jnp.where(scores[0] == gmax, 0.0,
                      jnp.where(scores[1] == gmax, 1.0, jnp.where(scores[2] == gmax, 2.0, 3.0)))
    in_grp = jnp.floor(eidx * (1.0 / EXPERTS_PER_GROUP)) == gstar
    cand = jnp.where(in_grp, sel, -jnp.inf)
    m1 = jnp.max(cand, axis=0, keepdims=True)
    idx1 = jnp.min(jnp.where(cand == m1, eidx, float(N_EXPERTS)), axis=0, keepdims=True)
    cand2 = jnp.where(eidx == idx1, -jnp.inf, cand)
    m2 = jnp.max(cand2, axis=0, keepdims=True)
    idx2 = jnp.min(jnp.where(cand2 == m2, eidx, float(N_EXPERTS)), axis=0, keepdims=True)
    lo, hi = jnp.minimum(idx1, idx2), jnp.maximum(idx1, idx2)
    a_lo = jnp.sum(jnp.where(eidx == lo, aff, 0.0), axis=0, keepdims=True)
    a_hi = jnp.sum(jnp.where(eidx == hi, aff, 0.0), axis=0, keepdims=True)
    den = a_lo + a_hi
    i_in = lo - EXPERTS_PER_GROUP * gstar
    j_in = hi - EXPERTS_PER_GROUP * gstar
    first = jnp.where(i_in == 0.0, 0.0, jnp.where(i_in == 1.0, 2.0, 3.0))
    cls = gstar * len(PAIRS) + first + j_in - 1.0
    pad = jnp.zeros((ROUTE_ROWS - 3, logits.shape[1]), F32)
    return jnp.concatenate([cls, a_lo / den, a_hi / den, pad], axis=0)


def _post_kernel(x_ref, aT_ref, p_ref, pprev_ref, pnext_ref, mod_ref, wot_ref, wob_ref, pw_ref,
                 ps_ref, g2_ref, wr_ref, rb_ref, h2_in_ref, xmid_ref, h2_ref, route_ref, *, seq_len):
    del h2_in_ref
    tm = x_ref.shape[1]
    i = pl.program_id(1)
    n_tiles = seq_len // tm
    m = mod_ref[0]
    gate_msa = m[:, 2 * D_MODEL:3 * D_MODEL]
    shift2 = m[:, 3 * D_MODEL:4 * D_MODEL]
    scale2 = m[:, 4 * D_MODEL:5 * D_MODEL]

    pc = p_ref[0].astype(F32)
    prev = jnp.where(i > 0, pprev_ref[0].astype(F32), 0.0)
    nxt = jnp.where(i < n_tiles - 1, pnext_ref[0].astype(F32), 0.0)
    ext = jnp.concatenate([prev, pc, nxt], axis=0)
    t_seq = lax.broadcasted_iota(jnp.int32, (tm, POOL_GROUP_WIDTH), 0) + i * tm
    pools = []
    for g, w in enumerate(POOL_WINDOWS):
        lanes = slice(g * POOL_GROUP_WIDTH, (g + 1) * POOL_GROUP_WIDTH)
        a = ext[:, lanes]
        span = 1
        while span < w:
            a = a[:-span] + a[span:]
            span *= 2
        start = POOL_HALO - w // 2
        wsum = a[start:start + tm]
        lo = jnp.maximum(t_seq - w // 2, 0)
        hi = jnp.minimum(t_seq + w // 2, seq_len)
        pooled = wsum / (hi - lo).astype(F32) - pc[:, lanes]
        mixed = jnp.dot(pooled.astype(BF16), pw_ref[g], preferred_element_type=F32)
        pools.append(mixed * ps_ref[:, lanes])
    pool = jnp.concatenate(pools, axis=1).astype(BF16)

    proj = lax.dot_general(aT_ref[0], wot_ref[...], TN_DIMS, preferred_element_type=F32)
    proj = proj + jnp.dot(pool, wob_ref[...], preferred_element_type=F32)
    xm = x_ref[0] + gate_msa * proj
    xmid_ref[0] = xm

    h2 = _rms_modulate(xm, g2_ref[...], shift2, scale2)
    h2_ref[...] = h2
    h_hi = h2.astype(BF16)
    h_lo = (h2 - h_hi.astype(F32)).astype(BF16)
    r = lax.dot_general(wr_ref[...], jnp.concatenate([h_hi, h_lo], axis=0), NT_DIMS,
                        preferred_element_type=F32)
    logits = r[0:N_EXPERTS, 0:tm] + r[0:N_EXPERTS, tm:2 * tm] + r[N_EXPERTS:2 * N_EXPERTS, 0:tm]
    route_ref[0] = _route(logits, rb_ref[...])


def _post(x, aT, p, h2_buf, row_off, mod_l, row0, per_batch_mod, wo_top, wo_bot, pool_w,
          pool_scale, g2, wr2, rb, tm):
    b, t, _ = x.shape
    n_total = h2_buf.shape[0]
    hb = tm // POOL_HALO
    last = t // POOL_HALO - 1
    blk0, per_b = row_off // tm, t // tm
    mod_map = (lambda bi, i: (row0 + bi, 0, 0)) if per_batch_mod else (lambda bi, i: (row0, 0, 0))
    const2 = lambda bi, i: (0, 0)
    tok_spec = pl.BlockSpec((1, tm, D_MODEL), lambda bi, i: (bi, i, 0))
    in_specs = [
        tok_spec,
        pl.BlockSpec((1, ATTN_WIDTH, tm), lambda bi, i: (bi, 0, i)),
        pl.BlockSpec((1, tm, POOL_WIDTH), lambda bi, i: (bi, i, 0)),
        pl.BlockSpec((1, POOL_HALO, POOL_WIDTH), lambda bi, i: (bi, jnp.maximum(i * hb - 1, 0), 0)),
        pl.BlockSpec((1, POOL_HALO, POOL_WIDTH), lambda bi, i: (bi, jnp.minimum((i + 1) * hb, last), 0)),
        pl.BlockSpec((1, 1, 6 * D_MODEL), mod_map),
        pl.BlockSpec((ATTN_WIDTH, D_MODEL), const2),
        pl.BlockSpec((POOL_WIDTH, D_MODEL), const2),
        pl.BlockSpec((len(POOL_WINDOWS), POOL_GROUP_WIDTH, POOL_GROUP_WIDTH), lambda bi, i: (0, 0, 0)),
        pl.BlockSpec((1, POOL_WIDTH), const2),
        pl.BlockSpec((1, D_MODEL), const2),
        pl.BlockSpec((2 * N_EXPERTS, D_MODEL), const2),
        pl.BlockSpec((N_EXPERTS, 1), const2),
        pl.BlockSpec(memory_space=pl.ANY),
    ]
    args = [x, aT, p, p, p, mod_l, wo_top, wo_bot, pool_w, pool_scale, g2, wr2, rb, h2_buf]
    aliases = {len(args) - 1: 1}
    return pl.pallas_call(
        functools.partial(_post_kernel, seq_len=t),
        grid=(b, t // tm),
        in_specs=in_specs,
        out_specs=[
            tok_spec,
            pl.BlockSpec((tm, D_MODEL), lambda bi, i: (blk0 + bi * per_b + i, 0)),
            pl.BlockSpec((1, ROUTE_ROWS, tm), lambda bi, i: (bi, 0, i)),
        ],
        out_shape=[
            jax.ShapeDtypeStruct((b, t, D_MODEL), F32),
            jax.ShapeDtypeStruct((n_total, D_MODEL), F32),
            jax.ShapeDtypeStruct((b, ROUTE_ROWS, t), F32),
        ],
        input_output_aliases=aliases,
        compiler_params=_params(("arbitrary", "arbitrary")),
        name="post_attention",
    )(*args)


def _dispatch(route_rows, n_tokens, n_tiles):
    tile = MOE_TILE
    classes = jnp.arange(N_CLASSES, dtype=jnp.int32)
    cls = route_rows[:, 0].astype(jnp.int32)
    tok = jnp.arange(n_tokens, dtype=jnp.int32)
    _, tok_s, wlo_s, whi_s = lax.sort((cls, tok, route_rows[:, 1], route_rows[:, 2]),
                                      num_keys=1, is_stable=True)
    counts = jnp.sum((cls[:, None] == classes[None, :]).astype(jnp.int32), axis=0)
    starts = jnp.cumsum(counts) - counts
    tiles_per = (counts + tile - 1) // tile
    tile_end = jnp.cumsum(tiles_per)
    tile_start = tile_end - tiles_per
    t_ids = jnp.arange(n_tiles, dtype=jnp.int32)
    tile_cls = jnp.minimum(jnp.sum((tile_end[None, :] <= t_ids[:, None]).astype(jnp.int32), axis=1),
                           N_CLASSES - 1)
    onehot = tile_cls[:, None] == classes[None, :]

    def per_tile(table):
        return jnp.sum(jnp.where(onehot, table[None, :], 0), axis=1)

    grp, pair = tile_cls // len(PAIRS), tile_cls % len(PAIRS)
    pair_i = sum(jnp.where(pair == n, i, 0) for n, (i, _) in enumerate(PAIRS))
    pair_j = sum(jnp.where(pair == n, j, 0) for n, (_, j) in enumerate(PAIRS))
    ea = grp * EXPERTS_PER_GROUP + pair_i
    eb = grp * EXPERTS_PER_GROUP + pair_j
    r_ids = jnp.arange(tile, dtype=jnp.int32)[None, :]
    within = (t_ids - per_tile(tile_start))[:, None] * tile + r_ids
    valid = (within < per_tile(counts)[:, None]) & (within >= 0)
    k = jnp.clip(per_tile(starts)[:, None] + within, 0, n_tokens - 1)
    payload = jnp.stack([tok_s.astype(F32), wlo_s, whi_s], axis=1)[k]
    src = jnp.where(valid, payload[..., 0].astype(jnp.int32), 0)
    spare = n_tokens + (t_ids % 2)[:, None] * tile + r_ids
    dst = jnp.where(valid, src, spare)
    dst_prev = jnp.concatenate([n_tokens + tile + r_ids, dst[:-1]], axis=0)
    w = jnp.stack([jnp.where(valid, payload[..., 1], 0.0),
                   jnp.where(valid, payload[..., 2], 0.0)], axis=1)
    shape = (n_tiles, 1, tile)
    return ea, eb, src.reshape(shape), dst.reshape(shape), dst_prev.reshape(shape), w


N_DMA_CHUNKS = 8


def _experts_kernel(ea_ref, eb_ref, src_ref, srcn_ref, dst_ref, dstp_ref, w_ref, h_hbm,
                    wga_ref, wua_ref, wda_ref, wgb_ref, wub_ref, wdb_ref, y_in_hbm, y_hbm,
                    xbuf, ybuf, xb_ref, gsem, ssem):
    del ea_ref, eb_ref, y_in_hbm
    t = pl.program_id(0)
    n_t = pl.num_programs(0)
    tile = xbuf.shape[1]
    slot = t % 2
    other = 1 - slot
    chunk = tile // N_DMA_CHUNKS

    def gather_rows(idx_ref, s, rows):
        for r in rows:
            pltpu.make_async_copy(h_hbm.at[pl.ds(idx_ref[0, 0, r], 1), :],
                                  xbuf.at[s, pl.ds(r, 1), :], gsem.at[s]).start()

    def wait_gather(s):
        pltpu.make_async_copy(h_hbm.at[pl.ds(0, tile), :], xbuf.at[s], gsem.at[s]).wait()

    def scatter_rows(idx_ref, s, rows):
        for r in rows:
            pltpu.make_async_copy(ybuf.at[s, pl.ds(r, 1), :],
                                  y_hbm.at[pl.ds(idx_ref[0, 0, r], 1), :], ssem.at[s]).start()

    def wait_scatter(s):
        pltpu.make_async_copy(ybuf.at[s], y_hbm.at[pl.ds(0, tile), :], ssem.at[s]).wait()

    @pl.when(t == 0)
    def _():
        ybuf[...] = jnp.zeros(ybuf.shape, F32)
        gather_rows(src_ref, 0, range(tile))

    wait_gather(slot)
    xb_ref[...] = xbuf[slot].astype(BF16)
    x = xb_ref[...]
    wrow = w_ref[0]

    def issue(c):
        rows = range(c * chunk, (c + 1) * chunk)
        scatter_rows(dstp_ref, other, rows)
        gather_rows(srcn_ref, other, rows)

    def column(row):
        c = jnp.broadcast_to(row, (V7X_LANES, tile)).T
        return jnp.concatenate([c] * (D_EXPERT // V7X_LANES), axis=1)

    def hidden(wg_ref, wu_ref, gate, c0):
        issue(c0)
        a = jnp.dot(x, wg_ref[0, 0].astype(BF16), preferred_element_type=F32)
        issue(c0 + 1)
        u = jnp.dot(x, wu_ref[0, 0].astype(BF16), preferred_element_type=F32)
        return ((a * jax.nn.sigmoid(a)) * u * gate).astype(BF16)

    ha = hidden(wga_ref, wua_ref, column(wrow[0:1]), 0)
    hb = hidden(wgb_ref, wub_ref, column(wrow[1:2]), 2)
    issue(4)
    issue(5)
    y = jnp.dot(ha, wda_ref[0, 0].astype(BF16), preferred_element_type=F32)
    issue(6)
    issue(7)
    y = y + jnp.dot(hb, wdb_ref[0, 0].astype(BF16), preferred_element_type=F32)

    @pl.when(t >= 1)
    def _():
        wait_scatter(slot)

    ybuf[slot] = y

    @pl.when(t == n_t - 1)
    def _():
        scatter_rows(dst_ref, slot, range(tile))
        wait_scatter(other)
        wait_scatter(slot)
        wait_gather(other)


def _experts(h2_all, y_buf, disp, layer, wg, wu, wd):
    ea, eb, src, dst, dst_prev, w = disp
    n_tokens = h2_all.shape[0]
    n_tiles = src.shape[0]
    tile = MOE_TILE
    idx_spec = lambda fn: pl.BlockSpec((1, 1, tile), fn, memory_space=pltpu.SMEM)
    wspec = lambda shape, sel: pl.BlockSpec((1, 1) + shape, sel)
    grid_spec = pltpu.PrefetchScalarGridSpec(
        num_scalar_prefetch=2,
        grid=(n_tiles,),
        in_specs=[
            idx_spec(lambda t, ea, eb: (t, 0, 0)),
            idx_spec(lambda t, ea, eb: (jnp.minimum(t + 1, n_tiles - 1), 0, 0)),
            idx_spec(lambda t, ea, eb: (t, 0, 0)),
            idx_spec(lambda t, ea, eb: (t, 0, 0)),
            pl.BlockSpec((1, 2, tile), lambda t, ea, eb: (t, 0, 0)),
            pl.BlockSpec(memory_space=pl.ANY),
            wspec((D_MODEL, D_EXPERT), lambda t, ea, eb: (layer, ea[t], 0, 0)),
            wspec((D_MODEL, D_EXPERT), lambda t, ea, eb: (layer, ea[t], 0, 0)),
            wspec((D_EXPERT, D_MODEL), lambda t, ea, eb: (layer, ea[t], 0, 0)),
            wspec((D_MODEL, D_EXPERT), lambda t, ea, eb: (layer, eb[t], 0, 0)),
            wspec((D_MODEL, D_EXPERT), lambda t, ea, eb: (layer, eb[t], 0, 0)),
            wspec((D_EXPERT, D_MODEL), lambda t, ea, eb: (layer, eb[t], 0, 0)),
            pl.BlockSpec(memory_space=pl.ANY),
        ],
        out_specs=pl.BlockSpec(memory_space=pl.ANY),
        scratch_shapes=[
            pltpu.VMEM((2, tile, D_MODEL), F32),
            pltpu.VMEM((2, tile, D_MODEL), F32),
            pltpu.VMEM((tile, D_MODEL), BF16),
            pltpu.SemaphoreType.DMA((2,)),
            pltpu.SemaphoreType.DMA((2,)),
        ],
    )
    return pl.pallas_call(
        _experts_kernel,
        grid_spec=grid_spec,
        out_shape=jax.ShapeDtypeStruct((n_tokens + 2 * tile, D_MODEL), F32),
        input_output_aliases={14: 0},
        compiler_params=_params(("arbitrary",)),
        name="experts",
    )(ea, eb, src, src, dst, dst_prev, w, h2_all, wg, wu, wd, wg, wu, wd, y_buf)


def _rope_tables(t):
    half = HEAD_DIM // 2
    inv_freq = ROPE_THETA ** (-jnp.arange(0, half, 2, dtype=F32) / half)
    pos = jnp.arange(t, dtype=jnp.int32)
    ang_r = (pos // GRID_W).astype(F32)[None, :] * inv_freq[:, None]
    ang_c = (pos % GRID_W).astype(F32)[None, :] * inv_freq[:, None]
    return jnp.cos(ang_r), jnp.sin(ang_r), jnp.cos(ang_c), jnp.sin(ang_c)


def kernel(x_prompt, x_sample, cache_k, cache_v, c, c_ctx, norm1_g, norm2_g, ada_w, ada_b, w_in,
           q_norm_g, k_norm_g, pool_w, pool_scale, w_out, router_w, router_bias, expert_w_gate,
           expert_w_up, expert_w_down):
    n_ctx, t_ctx, _ = x_prompt.shape
    n_lat, t_lat, _ = x_sample.shape
    past = cache_k.shape[2]
    tok_ctx = n_ctx * t_ctx
    n_tokens = tok_ctx + n_lat * t_lat
    n_tiles = n_tokens // MOE_TILE + N_CLASSES
    tm_lat = 512

    cond = jnp.concatenate([c_ctx[None, :], c,
                            jnp.zeros((MOD_ROWS - 1 - n_lat, D_MODEL), F32)], axis=0)
    mod = _modulation(cond, ada_w, ada_b).reshape(DEPTH, MOD_ROWS, 1, 6 * D_MODEL)
    rope_tabs = _rope_tables(t_lat)

    wrT = router_w.T
    wr_hi = wrT.astype(BF16)
    wr2 = jnp.concatenate([wr_hi, (wrT - wr_hi.astype(F32)).astype(BF16)], axis=0)
    rb = router_bias.reshape(N_EXPERTS, 1)

    xp, xs = x_prompt, x_sample
    h2_all = jnp.zeros((n_tokens, D_MODEL), F32)
    y_all = jnp.zeros((n_tokens + 2 * MOE_TILE, D_MODEL), F32)
    new_k, new_v = [], []
    for l in range(DEPTH):
        w_in_b = w_in[l].astype(BF16)
        wqkvT = w_in_b[:, :QKV_WIDTH].T
        wp = w_in_b[:, QKV_WIDTH:]
        wo = w_out[l].astype(BF16)
        wo_top, wo_bot = wo[:ATTN_WIDTH], wo[ATTN_WIDTH:]
        pw = pool_w[l].astype(BF16)
        ps = pool_scale[l].reshape(1, POOL_WIDTH)
        g1 = norm1_g[l].reshape(1, D_MODEL)
        g2 = norm2_g[l].reshape(1, D_MODEL)
        gq = q_norm_g[l].reshape(HEAD_DIM, 1)
        gk = k_norm_g[l].reshape(HEAD_DIM, 1)
        mod_l = mod[l]
        res_ctx = None if l == 0 else (y_all, 0, mod[l - 1])
        res_lat = None if l == 0 else (y_all, tok_ctx, mod[l - 1])

        outs = _pre(xp, res_ctx, mod_l, 0, False, g1, wqkvT, wp, gq, gk, None, t_ctx, True)
        if l == 0:
            qT, k, vT, p, k32, v32 = outs
        else:
            qT, k, vT, p, xp, k32, v32 = outs
        aT = _attention(qT, k, vT, t_ctx, t_ctx, 1)
        xp, h2_all, route_ctx = _post(xp, aT, p, h2_all, 0, mod_l, 0, False, wo_top, wo_bot,
                                      pw, ps, g2, wr2, rb, t_ctx)
        new_k.append(k32.reshape(n_ctx, t_ctx, N_KV_HEADS, HEAD_DIM))
        new_v.append(v32.reshape(n_ctx, t_ctx, N_KV_HEADS, HEAD_DIM))

        outs = _pre(xs, res_lat, mod_l, 1, True, g1, wqkvT, wp, gq, gk, rope_tabs, tm_lat, False)
        if l == 0:
            qT, k, vT, p = outs
        else:
            qT, k, vT, p, xs = outs
        ck = cache_k[:, l].reshape(n_lat, past, KV_WIDTH).astype(BF16)
        cvT = cache_v[:, l].reshape(n_lat, past, KV_WIDTH).astype(BF16).transpose(0, 2, 1)
        k_all = jnp.concatenate([ck, k], axis=1)
        vT_all = jnp.concatenate([cvT, vT], axis=2)
        aT = _attention(qT, k_all, vT_all, 256, 512, 3)
        xs, h2_all, route_lat = _post(xs, aT, p, h2_all, tok_ctx, mod_l, 1, True, wo_top,
                                      wo_bot, pw, ps, g2, wr2, rb, 256)

        route_rows = jnp.concatenate(
            [route_ctx.transpose(0, 2, 1).reshape(tok_ctx, ROUTE_ROWS),
             route_lat.transpose(0, 2, 1).reshape(n_tokens - tok_ctx, ROUTE_ROWS)], axis=0)
        y_all = _experts(h2_all, y_all, _dispatch(route_rows, n_tokens, n_tiles), l,
                         expert_w_gate, expert_w_up, expert_w_down)

    mod_l = mod[DEPTH - 1]
    xp = _finish(xp, y_all, 0, mod_l, 0, False, t_ctx)
    xs = _finish(xs, y_all, tok_ctx, mod_l, 1, True, tm_lat)
    return xp, xs, jnp.stack(new_k, axis=1), jnp.stack(new_v, axis=1)
```

```python
import functools

import jax
import jax.numpy as jnp
from jax import lax
from jax.experimental import pallas as pl
from jax.experimental.pallas import tpu as pltpu

D_MODEL = 1024
DEPTH = 4
GRID_W = 64
HEAD_DIM = 64
N_Q_HEADS = 8
N_KV_HEADS = 2
Q_PER_KV = N_Q_HEADS // N_KV_HEADS
ATTN_WIDTH = N_Q_HEADS * HEAD_DIM
KV_WIDTH = N_KV_HEADS * HEAD_DIM
QKV_WIDTH = ATTN_WIDTH + 2 * KV_WIDTH
POOL_WINDOWS = (2, 4, 8, 16)
POOL_WIDTH = D_MODEL - ATTN_WIDTH
POOL_GROUP_WIDTH = POOL_WIDTH // len(POOL_WINDOWS)
N_EXPERTS = 16
N_EXPERT_GROUPS = 4
EXPERTS_PER_GROUP = N_EXPERTS // N_EXPERT_GROUPS
D_EXPERT = 512
ROPE_THETA = 10000.0
EPS = 1e-6

Q_SCALE = HEAD_DIM ** -0.5 * 1.4426950408889634
DENOM_ROWS = 16
MOD_ROWS = 16
POOL_HALO = 16
V7X_LANES = 128
VMEM_LIMIT = 48 * 1024 * 1024

PAIRS = tuple((i, j) for i in range(EXPERTS_PER_GROUP) for j in range(i + 1, EXPERTS_PER_GROUP))
N_CLASSES = N_EXPERT_GROUPS * len(PAIRS)
ROUTE_ROWS = 8
MOE_TILE = 256

F32 = jnp.float32
BF16 = jnp.bfloat16
NT_DIMS = (((1,), (1,)), ((), ()))
TN_DIMS = (((0,), (0,)), ((), ()))


def _params(sem):
    return pltpu.CompilerParams(dimension_semantics=sem, vmem_limit_bytes=VMEM_LIMIT)


def _mod_kernel(c_ref, w_ref, b_ref, o_ref):
    c = c_ref[...]
    a = c * jax.nn.sigmoid(c)
    o_ref[0] = jnp.dot(a, w_ref[0], preferred_element_type=F32,
                       precision=lax.Precision.HIGHEST) + b_ref[0]


def _modulation(cond, ada_w, ada_b):
    tn = 1536
    n_out = ada_w.shape[2]
    return pl.pallas_call(
        _mod_kernel,
        grid=(DEPTH, n_out // tn),
        in_specs=[
            pl.BlockSpec((MOD_ROWS, D_MODEL), lambda l, n: (0, 0)),
            pl.BlockSpec((1, D_MODEL, tn), lambda l, n: (l, 0, n)),
            pl.BlockSpec((1, 1, tn), lambda l, n: (l, 0, n)),
        ],
        out_specs=pl.BlockSpec((1, MOD_ROWS, tn), lambda l, n: (l, 0, n)),
        out_shape=jax.ShapeDtypeStruct((DEPTH, MOD_ROWS, n_out), F32),
        compiler_params=_params(("arbitrary", "arbitrary")),
        name="modulation",
    )(cond, ada_w, ada_b.reshape(DEPTH, 1, n_out))


def _cast_kernel(x_ref, o_ref):
    o_ref[...] = x_ref[...].astype(o_ref.dtype)


def _to_bf16(w):
    blk = pl.BlockSpec((1, 1) + w.shape[2:], lambda l, e: (l, e, 0, 0))
    return pl.pallas_call(
        _cast_kernel, grid=w.shape[:2], in_specs=[blk], out_specs=blk,
        out_shape=jax.ShapeDtypeStruct(w.shape, BF16),
        compiler_params=_params(("parallel", "parallel")), name="cast_bf16")(w)


def _rms_modulate(x, g, shift, scale):
    ms = jnp.mean(x * x, axis=-1, keepdims=True)
    h = x * lax.rsqrt(ms + EPS) * g
    return h * (1.0 + scale) + shift


def _pre_kernel(*refs, rope, emit_f32, residual):
    refs = list(refs)
    x_ref = refs.pop(0)
    if residual:
        y_ref, modp_ref = refs.pop(0), refs.pop(0)
    mod_ref, g1_ref, wqkv_ref, wp_ref, gq_ref, gk_ref = refs[:6]
    refs = refs[6:]
    if rope:
        cr_ref, sr_ref, cc_ref, sc_ref = refs[:4]
        refs = refs[4:]
    qT_ref, k_ref, vT_ref, p_ref = refs[:4]
    refs = refs[4:]
    tm = x_ref.shape[1]
    x = x_ref[0]
    if residual:
        x = x + modp_ref[0][:, 5 * D_MODEL:6 * D_MODEL] * y_ref[...]
        refs.pop(0)[0] = x
    m = mod_ref[0]
    h = _rms_modulate(x, g1_ref[...], m[:, 0:D_MODEL], m[:, D_MODEL:2 * D_MODEL])
    hb = h.astype(BF16)
    zT = lax.dot_general(wqkv_ref[...], hb, NT_DIMS, preferred_element_type=F32)
    p_ref[0] = jnp.dot(hb, wp_ref[...], preferred_element_type=F32).astype(BF16)

    def norm_rope(zt, n_heads, g_ref):
        z3 = zt.reshape(n_heads, HEAD_DIM, tm)
        ms = jnp.mean(z3 * z3, axis=1, keepdims=True)
        y = z3 * lax.rsqrt(ms + EPS) * g_ref[...][None]
        if rope:
            cr, sr = cr_ref[...][None], sr_ref[...][None]
            cc, sc = cc_ref[...][None], sc_ref[...][None]
            x1r, x2r, x1c, x2c = y[:, 0:16], y[:, 16:32], y[:, 32:48], y[:, 48:64]
            y = jnp.concatenate([x1r * cr - x2r * sr, x1r * sr + x2r * cr,
                                 x1c * cc - x2c * sc, x1c * sc + x2c * cc], axis=1)
        return y.reshape(n_heads * HEAD_DIM, tm)

    qT = norm_rope(zT[0:ATTN_WIDTH], N_Q_HEADS, gq_ref)
    qT_ref[0] = (qT * Q_SCALE).astype(BF16)
    kT = norm_rope(zT[ATTN_WIDTH:ATTN_WIDTH + KV_WIDTH], N_KV_HEADS, gk_ref)
    k_tok = kT.T
    k_ref[0] = k_tok.astype(BF16)
    vT = zT[ATTN_WIDTH + KV_WIDTH:QKV_WIDTH]
    vT_ref[0] = vT.astype(BF16)
    if emit_f32:
        k32_ref, v32_ref = refs[:2]
        k32_ref[0] = k_tok
        v32_ref[0] = vT.T


def _pre(x, res, mod_l, row0, per_batch_mod, g1, wqkvT, wp, gq, gk, rope_tabs, tm, emit_f32):
    b, t, _ = x.shape
    rope = rope_tabs is not None
    mod_map = (lambda bi, i: (row0 + bi, 0, 0)) if per_batch_mod else (lambda bi, i: (row0, 0, 0))
    tok_spec = pl.BlockSpec((1, tm, D_MODEL), lambda bi, i: (bi, i, 0))
    in_specs, args = [tok_spec], [x]
    if res is not None:
        y_all, row_off, mod_prev = res
        blk0, per_b = row_off // tm, t // tm
        in_specs += [pl.BlockSpec((tm, D_MODEL), lambda bi, i: (blk0 + bi * per_b + i, 0)),
                     pl.BlockSpec((1, 1, 6 * D_MODEL), mod_map)]
        args += [y_all, mod_prev]
    in_specs += [
        pl.BlockSpec((1, 1, 6 * D_MODEL), mod_map),
        pl.BlockSpec((1, D_MODEL), lambda bi, i: (0, 0)),
        pl.BlockSpec((QKV_WIDTH, D_MODEL), lambda bi, i: (0, 0)),
        pl.BlockSpec((D_MODEL, POOL_WIDTH), lambda bi, i: (0, 0)),
        pl.BlockSpec((HEAD_DIM, 1), lambda bi, i: (0, 0)),
        pl.BlockSpec((HEAD_DIM, 1), lambda bi, i: (0, 0)),
    ]
    args += [mod_l, g1, wqkvT, wp, gq, gk]
    if rope:
        in_specs += [pl.BlockSpec((16, tm), lambda bi, i: (0, i))] * 4
        args += list(rope_tabs)
    out_shape = [
        jax.ShapeDtypeStruct((b, ATTN_WIDTH, t), BF16),
        jax.ShapeDtypeStruct((b, t, KV_WIDTH), BF16),
        jax.ShapeDtypeStruct((b, KV_WIDTH, t), BF16),
        jax.ShapeDtypeStruct((b, t, POOL_WIDTH), BF16),
    ]
    out_specs = [
        pl.BlockSpec((1, ATTN_WIDTH, tm), lambda bi, i: (bi, 0, i)),
        pl.BlockSpec((1, tm, KV_WIDTH), lambda bi, i: (bi, i, 0)),
        pl.BlockSpec((1, KV_WIDTH, tm), lambda bi, i: (bi, 0, i)),
        pl.BlockSpec((1, tm, POOL_WIDTH), lambda bi, i: (bi, i, 0)),
    ]
    if res is not None:
        out_shape.append(jax.ShapeDtypeStruct((b, t, D_MODEL), F32))
        out_specs.append(tok_spec)
    if emit_f32:
        out_shape += [jax.ShapeDtypeStruct((b, t, KV_WIDTH), F32)] * 2
        out_specs += [pl.BlockSpec((1, tm, KV_WIDTH), lambda bi, i: (bi, i, 0))] * 2
    return pl.pallas_call(
        functools.partial(_pre_kernel, rope=rope, emit_f32=emit_f32, residual=res is not None),
        grid=(b, t // tm),
        in_specs=in_specs,
        out_specs=out_specs,
        out_shape=out_shape,
        compiler_params=_params(("parallel", "parallel")),
        name="pre_attention",
    )(*args)


def _finish_kernel(x_ref, y_ref, mod_ref, o_ref):
    o_ref[0] = x_ref[0] + mod_ref[0][:, 5 * D_MODEL:6 * D_MODEL] * y_ref[...]


def _finish(xmid, y_all, row_off, mod_l, row0, per_batch_mod, tm):
    b, t, _ = xmid.shape
    blk0, per_b = row_off // tm, t // tm
    mod_map = (lambda bi, i: (row0 + bi, 0, 0)) if per_batch_mod else (lambda bi, i: (row0, 0, 0))
    tok_spec = pl.BlockSpec((1, tm, D_MODEL), lambda bi, i: (bi, i, 0))
    return pl.pallas_call(
        _finish_kernel,
        grid=(b, t // tm),
        in_specs=[tok_spec,
                  pl.BlockSpec((tm, D_MODEL), lambda bi, i: (blk0 + bi * per_b + i, 0)),
                  pl.BlockSpec((1, 1, 6 * D_MODEL), mod_map)],
        out_specs=tok_spec,
        out_shape=jax.ShapeDtypeStruct((b, t, D_MODEL), F32),
        compiler_params=_params(("parallel", "parallel")),
        name="moe_residual",
    )(xmid, y_all, mod_l)


def _attn_kernel(qT_ref, k_ref, vT_ref, o_ref, qpad_ref, acc_ref, m_ref, *, tk, unroll):
    tq = qT_ref.shape[2]
    s_len = k_ref.shape[1]
    n_pairs = N_Q_HEADS // 2
    zeros = jnp.zeros((HEAD_DIM, tq), BF16)
    for h in range(N_Q_HEADS):
        qh = qT_ref[0, h * HEAD_DIM:(h + 1) * HEAD_DIM, :]
        qpad_ref[h // 2, :, (h % 2) * tq:(h % 2 + 1) * tq] = jnp.concatenate(
            [qh, zeros] if h < Q_PER_KV else [zeros, qh], axis=0)
    m_ref[...] = jnp.full(m_ref.shape, -jnp.inf, F32)
    acc_ref[...] = jnp.zeros(acc_ref.shape, F32)

    n_items = unroll * n_pairs

    def step(t, carry):
        ones = jnp.ones((DENOM_ROWS, tk), BF16)

        def keys(u):
            off = pl.multiple_of((t * unroll + u) * tk, tk)
            return k_ref[0, pl.ds(off, tk), :]

        def values(u, j):
            off = pl.multiple_of((t * unroll + u) * tk, tk)
            vb = vT_ref[0, j * HEAD_DIM:(j + 1) * HEAD_DIM, pl.ds(off, tk)]
            return jnp.concatenate([vb, ones], axis=0)

        def scores(i):
            return jnp.dot(keys(i // n_pairs), qpad_ref[i % n_pairs],
                           preferred_element_type=F32).astype(BF16)

        queue = [scores(0), scores(1)]
        for i in range(n_items):
            u, pr = divmod(i, n_pairs)
            j = (2 * pr) // Q_PER_KV
            s = queue.pop(0)
            m_old = m_ref[pr]
            m_new = jnp.maximum(m_old, jnp.max(s, axis=0, keepdims=True).astype(F32))
            alpha = jnp.exp2(m_old - m_new)
            p = jnp.exp2(s - m_new.astype(BF16))
            if i + 2 < n_items:
                queue.append(scores(i + 2))
            pv = jnp.dot(values(u, j), p, preferred_element_type=F32)
            acc_ref[pr] = alpha * acc_ref[pr] + pv
            m_ref[pr] = m_new
        return carry

    lax.fori_loop(0, s_len // (tk * unroll), step, 0)
    for h in range(N_Q_HEADS):
        lanes = slice((h % 2) * tq, (h % 2 + 1) * tq)
        o_ref[0, h * HEAD_DIM:(h + 1) * HEAD_DIM, :] = (
            acc_ref[h // 2, 0:HEAD_DIM, lanes]
            / acc_ref[h // 2, HEAD_DIM:HEAD_DIM + 1, lanes]).astype(BF16)


def _attention(qT, k_all, vT_all, tq, tk, unroll):
    b, _, t = qT.shape
    s_len = k_all.shape[1]
    return pl.pallas_call(
        functools.partial(_attn_kernel, tk=tk, unroll=unroll),
        grid=(b, t // tq),
        in_specs=[
            pl.BlockSpec((1, ATTN_WIDTH, tq), lambda bi, i: (bi, 0, i)),
            pl.BlockSpec((1, s_len, KV_WIDTH), lambda bi, i: (bi, 0, 0)),
            pl.BlockSpec((1, KV_WIDTH, s_len), lambda bi, i: (bi, 0, 0)),
        ],
        out_specs=pl.BlockSpec((1, ATTN_WIDTH, tq), lambda bi, i: (bi, 0, i)),
        out_shape=jax.ShapeDtypeStruct((b, ATTN_WIDTH, t), BF16),
        scratch_shapes=[
            pltpu.VMEM((N_Q_HEADS // 2, 2 * HEAD_DIM, 2 * tq), BF16),
            pltpu.VMEM((N_Q_HEADS // 2, HEAD_DIM + DENOM_ROWS, 2 * tq), F32),
            pltpu.VMEM((N_Q_HEADS // 2, 1, 2 * tq), F32),
        ],
        compiler_params=_params(("parallel", "parallel")),
        name="attention",
    )(qT, k_all, vT_all)


def _route(logits, bias):
    aff = jax.nn.sigmoid(logits)
    sel = aff + bias
    eidx = lax.broadcasted_iota(jnp.int32, sel.shape, 0).astype(F32)
    scores = []
    for g in range(N_EXPERT_GROUPS):
        r = [sel[EXPERTS_PER_GROUP * g + i:EXPERTS_PER_GROUP * g + i + 1, :]
             for i in range(EXPERTS_PER_GROUP)]
        best = None
        for i, j in PAIRS:
            sij = r[i] + r[j]
            best = sij if best is None else jnp.maximum(best, sij)
        scores.append(best)
    gmax = functools.reduce(jnp.maximum, scores)
    gstar = jnp.where(scores[0] == gmax, 0.0,
                      jnp.where(scores[1] == gmax, 1.0, jnp.where(scores[2] == gmax, 2.0, 3.0)))
    in_grp = jnp.floor(eidx * (1.0 / EXPERTS_PER_GROUP)) == gstar
    cand = jnp.where(in_grp, sel, -jnp.inf)
    m1 = jnp.max(cand, axis=0, keepdims=True)
    idx1 = jnp.min(jnp.where(cand == m1, eidx, float(N_EXPERTS)), axis=0, keepdims=True)
    cand2 = jnp.where(eidx == idx1, -jnp.inf, cand)
    m2 = jnp.max(cand2, axis=0, keepdims=True)
    idx2 = jnp.min(jnp.where(cand2 == m2, eidx, float(N_EXPERTS)), axis=0, keepdims=True)
    lo, hi = jnp.minimum(idx1, idx2), jnp.maximum(idx1, idx2)
    a_lo = jnp.sum(jnp.where(eidx == lo, aff, 0.0), axis=0, keepdims=True)
    a_hi = jnp.sum(jnp.where(eidx == hi, aff, 0.0), axis=0, keepdims=True)
    den = a_lo + a_hi
    i_in = lo - EXPERTS_PER_GROUP * gstar
    j_in = hi - EXPERTS_PER_GROUP * gstar
    first = jnp.where(i_in == 0.0, 0.0, jnp.where(i_in == 1.0, 2.0, 3.0))
    cls = gstar * len(PAIRS) + first + j_in - 1.0
    pad = jnp.zeros((ROUTE_ROWS - 3, logits.shape[1]), F32)
    return jnp.concatenate([cls, a_lo / den, a_hi / den, pad], axis=0)


def _post_kernel(x_ref, aT_ref, p_ref, pprev_ref, pnext_ref, mod_ref, wot_ref, wob_ref, pw_ref,
                 ps_ref, g2_ref, wr_ref, rb_ref, h2_in_ref, xmid_ref, h2_ref, route_ref, *, seq_len):
    del h2_in_ref
    tm = x_ref.shape[1]
    i = pl.program_id(1)
    n_tiles = seq_len // tm
    m = mod_ref[0]
    gate_msa = m[:, 2 * D_MODEL:3 * D_MODEL]
    shift2 = m[:, 3 * D_MODEL:4 * D_MODEL]
    scale2 = m[:, 4 * D_MODEL:5 * D_MODEL]

    pc = p_ref[0].astype(F32)
    prev = jnp.where(i > 0, pprev_ref[0].astype(F32), 0.0)
    nxt = jnp.where(i < n_tiles - 1, pnext_ref[0].astype(F32), 0.0)
    ext = jnp.concatenate([prev, pc, nxt], axis=0)
    t_seq = lax.broadcasted_iota(jnp.int32, (tm, POOL_GROUP_WIDTH), 0) + i * tm
    pools = []
    for g, w in enumerate(POOL_WINDOWS):
        lanes = slice(g * POOL_GROUP_WIDTH, (g + 1) * POOL_GROUP_WIDTH)
        a = ext[:, lanes]
        span = 1
        while span < w:
            a = a[:-span] + a[span:]
            span *= 2
        start = POOL_HALO - w // 2
        wsum = a[start:start + tm]
        lo = jnp.maximum(t_seq - w // 2, 0)
        hi = jnp.minimum(t_seq + w // 2, seq_len)
        pooled = wsum / (hi - lo).astype(F32) - pc[:, lanes]
        mixed = jnp.dot(pooled.astype(BF16), pw_ref[g], preferred_element_type=F32)
        pools.append(mixed * ps_ref[:, lanes])
    pool = jnp.concatenate(pools, axis=1).astype(BF16)

    proj = lax.dot_general(aT_ref[0], wot_ref[...], TN_DIMS, preferred_element_type=F32)
    proj = proj + jnp.dot(pool, wob_ref[...], preferred_element_type=F32)
    xm = x_ref[0] + gate_msa * proj
    xmid_ref[0] = xm

    h2 = _rms_modulate(xm, g2_ref[...], shift2, scale2)
    h2_ref[...] = h2
    h_hi = h2.astype(BF16)
    h_lo = (h2 - h_hi.astype(F32)).astype(BF16)
    r = lax.dot_general(wr_ref[...], jnp.concatenate([h_hi, h_lo], axis=0), NT_DIMS,
                        preferred_element_type=F32)
    logits = r[0:N_EXPERTS, 0:tm] + r[0:N_EXPERTS, tm:2 * tm] + r[N_EXPERTS:2 * N_EXPERTS, 0:tm]
    route_ref[0] = _route(logits, rb_ref[...])


def _post(x, aT, p, h2_buf, row_off, mod_l, row0, per_batch_mod, wo_top, wo_bot, pool_w,
          pool_scale, g2, wr2, rb, tm):
    b, t, _ = x.shape
    n_total = h2_buf.shape[0]
    hb = tm // POOL_HALO
    last = t // POOL_HALO - 1
    blk0, per_b = row_off // tm, t // tm
    mod_map = (lambda bi, i: (row0 + bi, 0, 0)) if per_batch_mod else (lambda bi, i: (row0, 0, 0))
    const2 = lambda bi, i: (0, 0)
    tok_spec = pl.BlockSpec((1, tm, D_MODEL), lambda bi, i: (bi, i, 0))
    in_specs = [
        tok_spec,
        pl.BlockSpec((1, ATTN_WIDTH, tm), lambda bi, i: (bi, 0, i)),
        pl.BlockSpec((1, tm, POOL_WIDTH), lambda bi, i: (bi, i, 0)),
        pl.BlockSpec((1, POOL_HALO, POOL_WIDTH), lambda bi, i: (bi, jnp.maximum(i * hb - 1, 0), 0)),
        pl.BlockSpec((1, POOL_HALO, POOL_WIDTH), lambda bi, i: (bi, jnp.minimum((i + 1) * hb, last), 0)),
        pl.BlockSpec((1, 1, 6 * D_MODEL), mod_map),
        pl.BlockSpec((ATTN_WIDTH, D_MODEL), const2),
        pl.BlockSpec((POOL_WIDTH, D_MODEL), const2),
        pl.BlockSpec((len(POOL_WINDOWS), POOL_GROUP_WIDTH, POOL_GROUP_WIDTH), lambda bi, i: (0, 0, 0)),
        pl.BlockSpec((1, POOL_WIDTH), const2),
        pl.BlockSpec((1, D_MODEL), const2),
        pl.BlockSpec((2 * N_EXPERTS, D_MODEL), const2),
        pl.BlockSpec((N_EXPERTS, 1), const2),
        pl.BlockSpec(memory_space=pl.ANY),
    ]
    args = [x, aT, p, p, p, mod_l, wo_top, wo_bot, pool_w, pool_scale, g2, wr2, rb, h2_buf]
    aliases = {len(args) - 1: 1}
    return pl.pallas_call(
        functools.partial(_post_kernel, seq_len=t),
        grid=(b, t // tm),
        in_specs=in_specs,
        out_specs=[
            tok_spec,
            pl.BlockSpec((tm, D_MODEL), lambda bi, i: (blk0 + bi * per_b + i, 0)),
            pl.BlockSpec((1, ROUTE_ROWS, tm), lambda bi, i: (bi, 0, i)),
        ],
        out_shape=[
            jax.ShapeDtypeStruct((b, t, D_MODEL), F32),
            jax.ShapeDtypeStruct((n_total, D_MODEL), F32),
            jax.ShapeDtypeStruct((b, ROUTE_ROWS, t), F32),
        ],
        input_output_aliases=aliases,
        compiler_params=_params(("arbitrary", "arbitrary")),
        name="post_attention",
    )(*args)


def _dispatch(route_rows, n_tokens, n_tiles):
    tile = MOE_TILE
    n_pad = n_tiles * tile - n_tokens
    classes = jnp.arange(N_CLASSES, dtype=jnp.int32)
    cls = route_rows[:, 0].astype(jnp.int32)
    counts = jnp.sum((cls[:, None] == classes[None, :]).astype(jnp.int32), axis=0)
    need_end = jnp.cumsum((-counts) % tile)
    pad_ids = jnp.arange(n_pad, dtype=jnp.int32)
    pad_cls = jnp.sum((need_end[None, :] <= pad_ids[:, None]).astype(jnp.int32), axis=1)
    cls_s, tok_s, wlo_s, whi_s = lax.sort(
        (jnp.concatenate([cls, pad_cls]),
         jnp.concatenate([jnp.arange(n_tokens, dtype=jnp.int32), jnp.full((n_pad,), -1, jnp.int32)]),
         jnp.concatenate([route_rows[:, 1], jnp.zeros((n_pad,), F32)]),
         jnp.concatenate([route_rows[:, 2], jnp.zeros((n_pad,), F32)])),
        num_keys=1, is_stable=True)
    tile_cls = jnp.minimum(cls_s.reshape(n_tiles, tile)[:, 0], N_CLASSES - 1)
    grp, pair = tile_cls // len(PAIRS), tile_cls % len(PAIRS)
    pair_i = sum(jnp.where(pair == n, i, 0) for n, (i, _) in enumerate(PAIRS))
    pair_j = sum(jnp.where(pair == n, j, 0) for n, (_, j) in enumerate(PAIRS))
    ea = grp * EXPERTS_PER_GROUP + pair_i
    eb = grp * EXPERTS_PER_GROUP + pair_j
    tok_s = tok_s.reshape(n_tiles, tile)
    valid = tok_s >= 0
    src = jnp.where(valid, tok_s, 0)
    t_ids = jnp.arange(n_tiles, dtype=jnp.int32)[:, None]
    r_ids = jnp.arange(tile, dtype=jnp.int32)[None, :]
    dst = jnp.where(valid, tok_s, n_tokens + (t_ids % 2) * tile + r_ids)
    dst_prev = jnp.concatenate([n_tokens + tile + r_ids, dst[:-1]], axis=0)
    w = jnp.stack([wlo_s.reshape(n_tiles, tile), whi_s.reshape(n_tiles, tile)], axis=1)
    shape = (n_tiles, 1, tile)
    return ea, eb, src.reshape(shape), dst.reshape(shape), dst_prev.reshape(shape), w


N_DMA_CHUNKS = 8


def _experts_kernel(ea_ref, eb_ref, src_ref, srcn_ref, dst_ref, dstp_ref, w_ref, h_hbm,
                    wga_ref, wua_ref, wda_ref, wgb_ref, wub_ref, wdb_ref, y_in_hbm, y_hbm,
                    xbuf, ybuf, xb_ref, ha_ref, gsem, ssem):
    del eb_ref, y_in_hbm
    t = pl.program_id(0)
    n_t = pl.num_programs(0)
    tile = xbuf.shape[1]
    slot = t % 2
    other = 1 - slot
    chunk = tile // N_DMA_CHUNKS

    def gather_rows(idx_ref, s, rows):
        for r in rows:
            pltpu.make_async_copy(h_hbm.at[pl.ds(idx_ref[0, 0, r], 1), :],
                                  xbuf.at[s, pl.ds(r, 1), :], gsem.at[s]).start()

    def wait_gather(s):
        pltpu.make_async_copy(h_hbm.at[pl.ds(0, tile), :], xbuf.at[s], gsem.at[s]).wait()

    def scatter_rows(idx_ref, s, rows):
        for r in rows:
            pltpu.make_async_copy(ybuf.at[s, pl.ds(r, 1), :],
                                  y_hbm.at[pl.ds(idx_ref[0, 0, r], 1), :], ssem.at[s]).start()

    def wait_scatter(s):
        pltpu.make_async_copy(ybuf.at[s], y_hbm.at[pl.ds(0, tile), :], ssem.at[s]).wait()

    @pl.when(t == 0)
    def _():
        ybuf[...] = jnp.zeros(ybuf.shape, F32)
        gather_rows(src_ref, 0, range(tile))

    wait_gather(slot)
    xb_ref[...] = xbuf[slot].astype(BF16)
    x = xb_ref[...]
    wrow = w_ref[0]

    def issue(c):
        rows = range(c * chunk, (c + 1) * chunk)
        scatter_rows(dstp_ref, other, rows)
        gather_rows(srcn_ref, other, rows)

    def column(row):
        c = jnp.broadcast_to(row, (V7X_LANES, tile)).T
        return jnp.concatenate([c] * (D_EXPERT // V7X_LANES), axis=1)

    def hidden(wg_ref, wu_ref, gate, c0):
        issue(c0)
        a = jnp.dot(x, wg_ref[0, 0], preferred_element_type=F32)
        issue(c0 + 1)
        u = jnp.dot(x, wu_ref[0, 0], preferred_element_type=F32)
        return ((a * jax.nn.sigmoid(a)) * u * gate).astype(BF16)

    always = ea_ref[t] >= 0

    @pl.when(always)
    def _():
        issue(0)
        issue(1)
        ha_ref[...] = hidden(wga_ref, wua_ref, column(wrow[0:1]), 2)

    @pl.when(t >= 1)
    def _():
        wait_scatter(slot)

    @pl.when(always)
    def _():
        hb = hidden(wgb_ref, wub_ref, column(wrow[1:2]), 4)
        issue(6)
        y = jnp.dot(ha_ref[...], wda_ref[0, 0], preferred_element_type=F32)
        issue(7)
        y = y + jnp.dot(hb, wdb_ref[0, 0], preferred_element_type=F32)
        ybuf[slot] = y

    @pl.when(t == n_t - 1)
    def _():
        scatter_rows(dst_ref, slot, range(tile))
        wait_scatter(other)
        wait_scatter(slot)
        wait_gather(other)


def _experts(h2_all, y_buf, disp, layer, wg, wu, wd):
    ea, eb, src, dst, dst_prev, w = disp
    n_tokens = h2_all.shape[0]
    n_tiles = src.shape[0]
    tile = MOE_TILE
    idx_spec = lambda fn: pl.BlockSpec((1, 1, tile), fn, memory_space=pltpu.SMEM)
    wspec = lambda shape, sel: pl.BlockSpec((1, 1) + shape, sel)
    grid_spec = pltpu.PrefetchScalarGridSpec(
        num_scalar_prefetch=2,
        grid=(n_tiles,),
        in_specs=[
            idx_spec(lambda t, ea, eb: (t, 0, 0)),
            idx_spec(lambda t, ea, eb: (jnp.minimum(t + 1, n_tiles - 1), 0, 0)),
            idx_spec(lambda t, ea, eb: (t, 0, 0)),
            idx_spec(lambda t, ea, eb: (t, 0, 0)),
            pl.BlockSpec((1, 2, tile), lambda t, ea, eb: (t, 0, 0)),
            pl.BlockSpec(memory_space=pl.ANY),
            wspec((D_MODEL, D_EXPERT), lambda t, ea, eb: (layer, ea[t], 0, 0)),
            wspec((D_MODEL, D_EXPERT), lambda t, ea, eb: (layer, ea[t], 0, 0)),
            wspec((D_EXPERT, D_MODEL), lambda t, ea, eb: (layer, ea[t], 0, 0)),
            wspec((D_MODEL, D_EXPERT), lambda t, ea, eb: (layer, eb[t], 0, 0)),
            wspec((D_MODEL, D_EXPERT), lambda t, ea, eb: (layer, eb[t], 0, 0)),
            wspec((D_EXPERT, D_MODEL), lambda t, ea, eb: (layer, eb[t], 0, 0)),
            pl.BlockSpec(memory_space=pl.ANY),
        ],
        out_specs=pl.BlockSpec(memory_space=pl.ANY),
        scratch_shapes=[
            pltpu.VMEM((2, tile, D_MODEL), F32),
            pltpu.VMEM((2, tile, D_MODEL), F32),
            pltpu.VMEM((tile, D_MODEL), BF16),
            pltpu.VMEM((tile, D_EXPERT), BF16),
            pltpu.SemaphoreType.DMA((2,)),
            pltpu.SemaphoreType.DMA((2,)),
        ],
    )
    return pl.pallas_call(
        _experts_kernel,
        grid_spec=grid_spec,
        out_shape=jax.ShapeDtypeStruct((n_tokens + 2 * tile, D_MODEL), F32),
        input_output_aliases={14: 0},
        compiler_params=_params(("arbitrary",)),
        name="experts",
    )(ea, eb, src, src, dst, dst_prev, w, h2_all, wg, wu, wd, wg, wu, wd, y_buf)


def _rope_tables(t):
    half = HEAD_DIM // 2
    inv_freq = ROPE_THETA ** (-jnp.arange(0, half, 2, dtype=F32) / half)
    pos = jnp.arange(t, dtype=jnp.int32)
    ang_r = (pos // GRID_W).astype(F32)[None, :] * inv_freq[:, None]
    ang_c = (pos % GRID_W).astype(F32)[None, :] * inv_freq[:, None]
    return jnp.cos(ang_r), jnp.sin(ang_r), jnp.cos(ang_c), jnp.sin(ang_c)


def kernel(x_prompt, x_sample, cache_k, cache_v, c, c_ctx, norm1_g, norm2_g, ada_w, ada_b, w_in,
           q_norm_g, k_norm_g, pool_w, pool_scale, w_out, router_w, router_bias, expert_w_gate,
           expert_w_up, expert_w_down):
    n_ctx, t_ctx, _ = x_prompt.shape
    n_lat, t_lat, _ = x_sample.shape
    past = cache_k.shape[2]
    tok_ctx = n_ctx * t_ctx
    n_tokens = tok_ctx + n_lat * t_lat
    n_tiles = n_tokens // MOE_TILE + N_CLASSES
    tm_lat = 512

    cond = jnp.concatenate([c_ctx[None, :], c,
                            jnp.zeros((MOD_ROWS - 1 - n_lat, D_MODEL), F32)], axis=0)
    mod = _modulation(cond, ada_w, ada_b).reshape(DEPTH, MOD_ROWS, 1, 6 * D_MODEL)
    rope_tabs = _rope_tables(t_lat)

    wrT = router_w.T
    wr_hi = wrT.astype(BF16)
    wr2 = jnp.concatenate([wr_hi, (wrT - wr_hi.astype(F32)).astype(BF16)], axis=0)
    rb = router_bias.reshape(N_EXPERTS, 1)
    wg, wu, wd = _to_bf16(expert_w_gate), _to_bf16(expert_w_up), _to_bf16(expert_w_down)

    xp, xs = x_prompt, x_sample
    h2_all = jnp.zeros((n_tokens, D_MODEL), F32)
    y_all = jnp.zeros((n_tokens + 2 * MOE_TILE, D_MODEL), F32)
    new_k, new_v = [], []
    for l in range(DEPTH):
        w_in_b = w_in[l].astype(BF16)
        wqkvT = w_in_b[:, :QKV_WIDTH].T
        wp = w_in_b[:, QKV_WIDTH:]
        wo = w_out[l].astype(BF16)
        wo_top, wo_bot = wo[:ATTN_WIDTH], wo[ATTN_WIDTH:]
        pw = pool_w[l].astype(BF16)
        ps = pool_scale[l].reshape(1, POOL_WIDTH)
        g1 = norm1_g[l].reshape(1, D_MODEL)
        g2 = norm2_g[l].reshape(1, D_MODEL)
        gq = q_norm_g[l].reshape(HEAD_DIM, 1)
        gk = k_norm_g[l].reshape(HEAD_DIM, 1)
        mod_l = mod[l]
        res_ctx = None if l == 0 else (y_all, 0, mod[l - 1])
        res_lat = None if l == 0 else (y_all, tok_ctx, mod[l - 1])

        outs = _pre(xp, res_ctx, mod_l, 0, False, g1, wqkvT, wp, gq, gk, None, t_ctx, True)
        if l == 0:
            qT, k, vT, p, k32, v32 = outs
        else:
            qT, k, vT, p, xp, k32, v32 = outs
        aT = _attention(qT, k, vT, t_ctx, t_ctx, 1)
        xp, h2_all, route_ctx = _post(xp, aT, p, h2_all, 0, mod_l, 0, False, wo_top, wo_bot,
                                      pw, ps, g2, wr2, rb, t_ctx)
        new_k.append(k32.reshape(n_ctx, t_ctx, N_KV_HEADS, HEAD_DIM))
        new_v.append(v32.reshape(n_ctx, t_ctx, N_KV_HEADS, HEAD_DIM))

        outs = _pre(xs, res_lat, mod_l, 1, True, g1, wqkvT, wp, gq, gk, rope_tabs, tm_lat, False)
        if l == 0:
            qT, k, vT, p = outs
        else:
            qT, k, vT, p, xs = outs
        ck = cache_k[:, l].reshape(n_lat, past, KV_WIDTH).astype(BF16)
        cvT = cache_v[:, l].reshape(n_lat, past, KV_WIDTH).astype(BF16).transpose(0, 2, 1)
        k_all = jnp.concatenate([ck, k], axis=1)
        vT_all = jnp.concatenate([cvT, vT], axis=2)
        aT = _attention(qT, k_all, vT_all, 256, 512, 3)
        xs, h2_all, route_lat = _post(xs, aT, p, h2_all, tok_ctx, mod_l, 1, True, wo_top,
                                      wo_bot, pw, ps, g2, wr2, rb, 256)

        route_rows = jnp.concatenate(
            [route_ctx.transpose(0, 2, 1).reshape(tok_ctx, ROUTE_ROWS),
             route_lat.transpose(0, 2, 1).reshape(n_tokens - tok_ctx, ROUTE_ROWS)], axis=0)
        y_all = _experts(h2_all, y_all, _dispatch(route_rows, n_tokens, n_tiles), l, wg, wu, wd)

    mod_l = mod[DEPTH - 1]
    xp = _finish(xp, y_all, 0, mod_l, 0, False, t_ctx)
    xs = _finish(xs, y_all, tok_ctx, mod_l, 1, True, tm_lat)
    return xp, xs, jnp.stack(new_k, axis=1), jnp.stack(new_v, axis=1)
```

```python
import functools

import jax
import jax.numpy as jnp
from jax import lax
from jax.experimental import pallas as pl
from jax.experimental.pallas import tpu as pltpu

D_MODEL = 1024
DEPTH = 4
GRID_W = 64
HEAD_DIM = 64
N_Q_HEADS = 8
N_KV_HEADS = 2
Q_PER_KV = N_Q_HEADS // N_KV_HEADS
ATTN_WIDTH = N_Q_HEADS * HEAD_DIM
KV_WIDTH = N_KV_HEADS * HEAD_DIM
QKV_WIDTH = ATTN_WIDTH + 2 * KV_WIDTH
POOL_WINDOWS = (2, 4, 8, 16)
POOL_WIDTH = D_MODEL - ATTN_WIDTH
POOL_GROUP_WIDTH = POOL_WIDTH // len(POOL_WINDOWS)
N_EXPERTS = 16
N_EXPERT_GROUPS = 4
EXPERTS_PER_GROUP = N_EXPERTS // N_EXPERT_GROUPS
D_EXPERT = 512
ROPE_THETA = 10000.0
EPS = 1e-6

Q_SCALE = HEAD_DIM ** -0.5 * 1.4426950408889634
DENOM_ROWS = 32
P_SHIFT = 8.0
MOD_ROWS = 16
POOL_HALO = 16
V7X_LANES = 128
VMEM_LIMIT = 48 * 1024 * 1024

PAIRS = tuple((i, j) for i in range(EXPERTS_PER_GROUP) for j in range(i + 1, EXPERTS_PER_GROUP))
N_CLASSES = N_EXPERT_GROUPS * len(PAIRS)
ROUTE_ROWS = 8
MOE_TILE = 256

F32 = jnp.float32
BF16 = jnp.bfloat16
QK8 = jnp.float8_e4m3fn
NT_DIMS = (((1,), (1,)), ((), ()))
TN_DIMS = (((0,), (0,)), ((), ()))


def _params(sem):
    return pltpu.CompilerParams(dimension_semantics=sem, vmem_limit_bytes=VMEM_LIMIT)


def _mod_kernel(c_ref, w_ref, b_ref, o_ref):
    c = c_ref[...]
    a = c * jax.nn.sigmoid(c)
    o_ref[0] = jnp.dot(a, w_ref[0], preferred_element_type=F32,
                       precision=lax.Precision.HIGHEST) + b_ref[0]


def _modulation(cond, ada_w, ada_b):
    tn = 1536
    n_out = ada_w.shape[2]
    return pl.pallas_call(
        _mod_kernel,
        grid=(DEPTH, n_out // tn),
        in_specs=[
            pl.BlockSpec((MOD_ROWS, D_MODEL), lambda l, n: (0, 0)),
            pl.BlockSpec((1, D_MODEL, tn), lambda l, n: (l, 0, n)),
            pl.BlockSpec((1, 1, tn), lambda l, n: (l, 0, n)),
        ],
        out_specs=pl.BlockSpec((1, MOD_ROWS, tn), lambda l, n: (l, 0, n)),
        out_shape=jax.ShapeDtypeStruct((DEPTH, MOD_ROWS, n_out), F32),
        compiler_params=_params(("arbitrary", "arbitrary")),
        name="modulation",
    )(cond, ada_w, ada_b.reshape(DEPTH, 1, n_out))


def _cast_kernel(x_ref, o_ref):
    o_ref[...] = x_ref[...].astype(o_ref.dtype)


def _to_bf16(w):
    blk = pl.BlockSpec((1, 1) + w.shape[2:], lambda l, e: (l, e, 0, 0))
    return pl.pallas_call(
        _cast_kernel, grid=w.shape[:2], in_specs=[blk], out_specs=blk,
        out_shape=jax.ShapeDtypeStruct(w.shape, BF16),
        compiler_params=_params(("parallel", "parallel")), name="cast_bf16")(w)


def _rms_modulate(x, g, shift, scale):
    ms = jnp.mean(x * x, axis=-1, keepdims=True)
    h = x * lax.rsqrt(ms + EPS) * g
    return h * (1.0 + scale) + shift


def _pre_kernel(*refs, rope, emit_f32, residual):
    refs = list(refs)
    x_ref = refs.pop(0)
    if residual:
        y_ref, modp_ref = refs.pop(0), refs.pop(0)
    mod_ref, g1_ref, wqkv_ref, wp_ref, gq_ref, gk_ref = refs[:6]
    refs = refs[6:]
    if rope:
        cr_ref, sr_ref, cc_ref, sc_ref = refs[:4]
        refs = refs[4:]
    qT_ref, k_ref, vT_ref, p_ref = refs[:4]
    refs = refs[4:]
    tm = x_ref.shape[1]
    x = x_ref[0]
    if residual:
        x = x + modp_ref[0][:, 5 * D_MODEL:6 * D_MODEL] * y_ref[...]
        refs.pop(0)[0] = x
    m = mod_ref[0]
    h = _rms_modulate(x, g1_ref[...], m[:, 0:D_MODEL], m[:, D_MODEL:2 * D_MODEL])
    hb = h.astype(BF16)
    zT = lax.dot_general(wqkv_ref[...], hb, NT_DIMS, preferred_element_type=F32)
    p_ref[0] = jnp.dot(hb, wp_ref[...], preferred_element_type=F32).astype(BF16)

    def norm_rope(zt, n_heads, g_ref):
        z3 = zt.reshape(n_heads, HEAD_DIM, tm)
        ms = jnp.mean(z3 * z3, axis=1, keepdims=True)
        y = z3 * lax.rsqrt(ms + EPS) * g_ref[...][None]
        if rope:
            cr, sr = cr_ref[...][None], sr_ref[...][None]
            cc, sc = cc_ref[...][None], sc_ref[...][None]
            x1r, x2r, x1c, x2c = y[:, 0:16], y[:, 16:32], y[:, 32:48], y[:, 48:64]
            y = jnp.concatenate([x1r * cr - x2r * sr, x1r * sr + x2r * cr,
                                 x1c * cc - x2c * sc, x1c * sc + x2c * cc], axis=1)
        return y.reshape(n_heads * HEAD_DIM, tm)

    qT = norm_rope(zT[0:ATTN_WIDTH], N_Q_HEADS, gq_ref)
    qT_ref[0] = (qT * Q_SCALE).astype(BF16)
    kT = norm_rope(zT[ATTN_WIDTH:ATTN_WIDTH + KV_WIDTH], N_KV_HEADS, gk_ref)
    k_tok = kT.T
    k_ref[0] = k_tok.astype(QK8)
    vT = zT[ATTN_WIDTH + KV_WIDTH:QKV_WIDTH]
    vT_ref[0] = vT.astype(QK8)
    if emit_f32:
        k32_ref, v32_ref = refs[:2]
        k32_ref[0] = k_tok
        v32_ref[0] = vT.T


def _pre(x, res, mod_l, row0, per_batch_mod, g1, wqkvT, wp, gq, gk, rope_tabs, tm, emit_f32):
    b, t, _ = x.shape
    rope = rope_tabs is not None
    mod_map = (lambda bi, i: (row0 + bi, 0, 0)) if per_batch_mod else (lambda bi, i: (row0, 0, 0))
    tok_spec = pl.BlockSpec((1, tm, D_MODEL), lambda bi, i: (bi, i, 0))
    in_specs, args = [tok_spec], [x]
    if res is not None:
        y_all, row_off, mod_prev = res
        blk0, per_b = row_off // tm, t // tm
        in_specs += [pl.BlockSpec((tm, D_MODEL), lambda bi, i: (blk0 + bi * per_b + i, 0)),
                     pl.BlockSpec((1, 1, 6 * D_MODEL), mod_map)]
        args += [y_all, mod_prev]
    in_specs += [
        pl.BlockSpec((1, 1, 6 * D_MODEL), mod_map),
        pl.BlockSpec((1, D_MODEL), lambda bi, i: (0, 0)),
        pl.BlockSpec((QKV_WIDTH, D_MODEL), lambda bi, i: (0, 0)),
        pl.BlockSpec((D_MODEL, POOL_WIDTH), lambda bi, i: (0, 0)),
        pl.BlockSpec((HEAD_DIM, 1), lambda bi, i: (0, 0)),
        pl.BlockSpec((HEAD_DIM, 1), lambda bi, i: (0, 0)),
    ]
    args += [mod_l, g1, wqkvT, wp, gq, gk]
    if rope:
        in_specs += [pl.BlockSpec((16, tm), lambda bi, i: (0, i))] * 4
        args += list(rope_tabs)
    out_shape = [
        jax.ShapeDtypeStruct((b, ATTN_WIDTH, t), BF16),
        jax.ShapeDtypeStruct((b, t, KV_WIDTH), QK8),
        jax.ShapeDtypeStruct((b, KV_WIDTH, t), QK8),
        jax.ShapeDtypeStruct((b, t, POOL_WIDTH), BF16),
    ]
    out_specs = [
        pl.BlockSpec((1, ATTN_WIDTH, tm), lambda bi, i: (bi, 0, i)),
        pl.BlockSpec((1, tm, KV_WIDTH), lambda bi, i: (bi, i, 0)),
        pl.BlockSpec((1, KV_WIDTH, tm), lambda bi, i: (bi, 0, i)),
        pl.BlockSpec((1, tm, POOL_WIDTH), lambda bi, i: (bi, i, 0)),
    ]
    if res is not None:
        out_shape.append(jax.ShapeDtypeStruct((b, t, D_MODEL), F32))
        out_specs.append(tok_spec)
    if emit_f32:
        out_shape += [jax.ShapeDtypeStruct((b, t, KV_WIDTH), F32)] * 2
        out_specs += [pl.BlockSpec((1, tm, KV_WIDTH), lambda bi, i: (bi, i, 0))] * 2
    return pl.pallas_call(
        functools.partial(_pre_kernel, rope=rope, emit_f32=emit_f32, residual=res is not None),
        grid=(b, t // tm),
        in_specs=in_specs,
        out_specs=out_specs,
        out_shape=out_shape,
        compiler_params=_params(("parallel", "parallel")),
        name="pre_attention",
    )(*args)


def _finish_kernel(x_ref, y_ref, mod_ref, o_ref):
    o_ref[0] = x_ref[0] + mod_ref[0][:, 5 * D_MODEL:6 * D_MODEL] * y_ref[...]


def _finish(xmid, y_all, row_off, mod_l, row0, per_batch_mod, tm):
    b, t, _ = xmid.shape
    blk0, per_b = row_off // tm, t // tm
    mod_map = (lambda bi, i: (row0 + bi, 0, 0)) if per_batch_mod else (lambda bi, i: (row0, 0, 0))
    tok_spec = pl.BlockSpec((1, tm, D_MODEL), lambda bi, i: (bi, i, 0))
    return pl.pallas_call(
        _finish_kernel,
        grid=(b, t // tm),
        in_specs=[tok_spec,
                  pl.BlockSpec((tm, D_MODEL), lambda bi, i: (blk0 + bi * per_b + i, 0)),
                  pl.BlockSpec((1, 1, 6 * D_MODEL), mod_map)],
        out_specs=tok_spec,
        out_shape=jax.ShapeDtypeStruct((b, t, D_MODEL), F32),
        compiler_params=_params(("parallel", "parallel")),
        name="moe_residual",
    )(xmid, y_all, mod_l)


def _attn_kernel(qT_ref, k_ref, vT_ref, o_ref, qpad_ref, acc_ref, m_ref, *, tk, unroll):
    tq = qT_ref.shape[2]
    s_len = k_ref.shape[1]
    n_pairs = N_Q_HEADS // 2
    zeros = jnp.zeros((HEAD_DIM, tq), QK8)
    for h in range(N_Q_HEADS):
        qh = qT_ref[0, h * HEAD_DIM:(h + 1) * HEAD_DIM, :].astype(QK8)
        qpad_ref[h // 2, :, (h % 2) * tq:(h % 2 + 1) * tq] = jnp.concatenate(
            [qh, zeros] if h < Q_PER_KV else [zeros, qh], axis=0)
    m_ref[...] = jnp.full(m_ref.shape, -jnp.inf, F32)
    acc_ref[...] = jnp.zeros(acc_ref.shape, F32)

    n_items = unroll * n_pairs

    def step(t, carry):
        ones = jnp.ones((DENOM_ROWS, tk), QK8)

        def keys(u):
            off = pl.multiple_of((t * unroll + u) * tk, tk)
            return k_ref[0, pl.ds(off, tk), :]

        def values(u, j):
            off = pl.multiple_of((t * unroll + u) * tk, tk)
            vb = vT_ref[0, j * HEAD_DIM:(j + 1) * HEAD_DIM, pl.ds(off, tk)]
            return jnp.concatenate([vb, ones], axis=0)

        def scores(i):
            return jnp.dot(keys(i // n_pairs), qpad_ref[i % n_pairs],
                           preferred_element_type=F32).astype(BF16)

        queue = [scores(0), scores(1)]
        for i in range(n_items):
            u, pr = divmod(i, n_pairs)
            j = (2 * pr) // Q_PER_KV
            s = queue.pop(0)
            m_old = m_ref[pr]
            m_new = jnp.maximum(m_old, jnp.max(s, axis=0, keepdims=True).astype(F32))
            c_old = (m_old - P_SHIFT).astype(BF16)
            c_new = (m_new - P_SHIFT).astype(BF16)
            alpha = jnp.exp2(c_old.astype(F32) - c_new.astype(F32))
            p = jnp.exp2(jnp.minimum(s - c_new, P_SHIFT)).astype(QK8)
            if i + 2 < n_items:
                queue.append(scores(i + 2))
            pv = jnp.dot(values(u, j), p, preferred_element_type=F32)
            acc_ref[pr] = alpha * acc_ref[pr] + pv
            m_ref[pr] = m_new
        return carry

    lax.fori_loop(0, s_len // (tk * unroll), step, 0)
    for h in range(N_Q_HEADS):
        lanes = slice((h % 2) * tq, (h % 2 + 1) * tq)
        o_ref[0, h * HEAD_DIM:(h + 1) * HEAD_DIM, :] = (
            acc_ref[h // 2, 0:HEAD_DIM, lanes]
            / acc_ref[h // 2, HEAD_DIM:HEAD_DIM + 1, lanes]).astype(BF16)


def _attention(qT, k_all, vT_all, tq, tk, unroll):
    b, _, t = qT.shape
    s_len = k_all.shape[1]
    return pl.pallas_call(
        functools.partial(_attn_kernel, tk=tk, unroll=unroll),
        grid=(b, t // tq),
        in_specs=[
            pl.BlockSpec((1, ATTN_WIDTH, tq), lambda bi, i: (bi, 0, i)),
            pl.BlockSpec((1, s_len, KV_WIDTH), lambda bi, i: (bi, 0, 0)),
            pl.BlockSpec((1, KV_WIDTH, s_len), lambda bi, i: (bi, 0, 0)),
        ],
        out_specs=pl.BlockSpec((1, ATTN_WIDTH, tq), lambda bi, i: (bi, 0, i)),
        out_shape=jax.ShapeDtypeStruct((b, ATTN_WIDTH, t), BF16),
        scratch_shapes=[
            pltpu.VMEM((N_Q_HEADS // 2, 2 * HEAD_DIM, 2 * tq), QK8),
            pltpu.VMEM((N_Q_HEADS // 2, HEAD_DIM + DENOM_ROWS, 2 * tq), F32),
            pltpu.VMEM((N_Q_HEADS // 2, 1, 2 * tq), F32),
        ],
        compiler_params=_params(("parallel", "parallel")),
        name="attention",
    )(qT, k_all, vT_all)


def _route(logits, bias):
    aff = jax.nn.sigmoid(logits)
    sel = aff + bias
    eidx = lax.broadcasted_iota(jnp.int32, sel.shape, 0).astype(F32)
    scores = []
    for g in range(N_EXPERT_GROUPS):
        r = [sel[EXPERTS_PER_GROUP * g + i:EXPERTS_PER_GROUP * g + i + 1, :]
             for i in range(EXPERTS_PER_GROUP)]
        best = None
        for i, j in PAIRS:
            sij = r[i] + r[j]
            best = sij if best is None else jnp.maximum(best, sij)
        scores.append(best)
    gmax = functools.reduce(jnp.maximum, scores)
    gstar = jnp.where(scores[0] == gmax, 0.0,
                      jnp.where(scores[1] == gmax, 1.0, jnp.where(scores[2] == gmax, 2.0, 3.0)))
    in_grp = jnp.floor(eidx * (1.0 / EXPERTS_PER_GROUP)) == gstar
    cand = jnp.where(in_grp, sel, -jnp.inf)
    m1 = jnp.max(cand, axis=0, keepdims=True)
    idx1 = jnp.min(jnp.where(cand == m1, eidx, float(N_EXPERTS)), axis=0, keepdims=True)
    cand2 = jnp.where(eidx == idx1, -jnp.inf, cand)
    m2 = jnp.max(cand2, axis=0, keepdims=True)
    idx2 = jnp.min(jnp.where(cand2 == m2, eidx, float(N_EXPERTS)), axis=0, keepdims=True)
    lo, hi = jnp.minimum(idx1, idx2), jnp.maximum(idx1, idx2)
    a_lo = jnp.sum(jnp.where(eidx == lo, aff, 0.0), axis=0, keepdims=True)
    a_hi = jnp.sum(jnp.where(eidx == hi, aff, 0.0), axis=0, keepdims=True)
    den = a_lo + a_hi
    i_in = lo - EXPERTS_PER_GROUP * gstar
    j_in = hi - EXPERTS_PER_GROUP * gstar
    first = jnp.where(i_in == 0.0, 0.0, jnp.where(i_in == 1.0, 2.0, 3.0))
    cls = gstar * len(PAIRS) + first + j_in - 1.0
    pad = jnp.zeros((ROUTE_ROWS - 3, logits.shape[1]), F32)
    return jnp.concatenate([cls, a_lo / den, a_hi / den, pad], axis=0)


def _post_kernel(x_ref, aT_ref, p_ref, pprev_ref, pnext_ref, mod_ref, wot_ref, wob_ref, pw_ref,
                 ps_ref, g2_ref, wr_ref, rb_ref, h2_in_ref, xmid_ref, h2_ref, route_ref, *, seq_len):
    del h2_in_ref
    tm = x_ref.shape[1]
    i = pl.program_id(1)
    n_tiles = seq_len // tm
    m = mod_ref[0]
    gate_msa = m[:, 2 * D_MODEL:3 * D_MODEL]
    shift2 = m[:, 3 * D_MODEL:4 * D_MODEL]
    scale2 = m[:, 4 * D_MODEL:5 * D_MODEL]

    pc = p_ref[0].astype(F32)
    prev = jnp.where(i > 0, pprev_ref[0].astype(F32), 0.0)
    nxt = jnp.where(i < n_tiles - 1, pnext_ref[0].astype(F32), 0.0)
    ext = jnp.concatenate([prev, pc, nxt], axis=0)
    t_seq = lax.broadcasted_iota(jnp.int32, (tm, POOL_GROUP_WIDTH), 0) + i * tm
    pools = []
    for g, w in enumerate(POOL_WINDOWS):
        lanes = slice(g * POOL_GROUP_WIDTH, (g + 1) * POOL_GROUP_WIDTH)
        a = ext[:, lanes]
        span = 1
        while span < w:
            a = a[:-span] + a[span:]
            span *= 2
        start = POOL_HALO - w // 2
        wsum = a[start:start + tm]
        lo = jnp.maximum(t_seq - w // 2, 0)
        hi = jnp.minimum(t_seq + w // 2, seq_len)
        pooled = wsum / (hi - lo).astype(F32) - pc[:, lanes]
        mixed = jnp.dot(pooled.astype(BF16), pw_ref[g], preferred_element_type=F32)
        pools.append(mixed * ps_ref[:, lanes])
    pool = jnp.concatenate(pools, axis=1).astype(BF16)

    proj = lax.dot_general(aT_ref[0], wot_ref[...], TN_DIMS, preferred_element_type=F32)
    proj = proj + jnp.dot(pool, wob_ref[...], preferred_element_type=F32)
    xm = x_ref[0] + gate_msa * proj
    xmid_ref[0] = xm

    h2 = _rms_modulate(xm, g2_ref[...], shift2, scale2)
    h2_ref[...] = h2
    h_hi = h2.astype(BF16)
    h_lo = (h2 - h_hi.astype(F32)).astype(BF16)
    r = lax.dot_general(wr_ref[...], jnp.concatenate([h_hi, h_lo], axis=0), NT_DIMS,
                        preferred_element_type=F32)
    logits = r[0:N_EXPERTS, 0:tm] + r[0:N_EXPERTS, tm:2 * tm] + r[N_EXPERTS:2 * N_EXPERTS, 0:tm]
    route_ref[0] = _route(logits, rb_ref[...])


def _post(x, aT, p, h2_buf, row_off, mod_l, row0, per_batch_mod, wo_top, wo_bot, pool_w,
          pool_scale, g2, wr2, rb, tm):
    b, t, _ = x.shape
    n_total = h2_buf.shape[0]
    hb = tm // POOL_HALO
    last = t // POOL_HALO - 1
    blk0, per_b = row_off // tm, t // tm
    mod_map = (lambda bi, i: (row0 + bi, 0, 0)) if per_batch_mod else (lambda bi, i: (row0, 0, 0))
    const2 = lambda bi, i: (0, 0)
    tok_spec = pl.BlockSpec((1, tm, D_MODEL), lambda bi, i: (bi, i, 0))
    in_specs = [
        tok_spec,
        pl.BlockSpec((1, ATTN_WIDTH, tm), lambda bi, i: (bi, 0, i)),
        pl.BlockSpec((1, tm, POOL_WIDTH), lambda bi, i: (bi, i, 0)),
        pl.BlockSpec((1, POOL_HALO, POOL_WIDTH), lambda bi, i: (bi, jnp.maximum(i * hb - 1, 0), 0)),
        pl.BlockSpec((1, POOL_HALO, POOL_WIDTH), lambda bi, i: (bi, jnp.minimum((i + 1) * hb, last), 0)),
        pl.BlockSpec((1, 1, 6 * D_MODEL), mod_map),
        pl.BlockSpec((ATTN_WIDTH, D_MODEL), const2),
        pl.BlockSpec((POOL_WIDTH, D_MODEL), const2),
        pl.BlockSpec((len(POOL_WINDOWS), POOL_GROUP_WIDTH, POOL_GROUP_WIDTH), lambda bi, i: (0, 0, 0)),
        pl.BlockSpec((1, POOL_WIDTH), const2),
        pl.BlockSpec((1, D_MODEL), const2),
        pl.BlockSpec((2 * N_EXPERTS, D_MODEL), const2),
        pl.BlockSpec((N_EXPERTS, 1), const2),
        pl.BlockSpec(memory_space=pl.ANY),
    ]
    args = [x, aT, p, p, p, mod_l, wo_top, wo_bot, pool_w, pool_scale, g2, wr2, rb, h2_buf]
    aliases = {len(args) - 1: 1}
    return pl.pallas_call(
        functools.partial(_post_kernel, seq_len=t),
        grid=(b, t // tm),
        in_specs=in_specs,
        out_specs=[
            tok_spec,
            pl.BlockSpec((tm, D_MODEL), lambda bi, i: (blk0 + bi * per_b + i, 0)),
            pl.BlockSpec((1, ROUTE_ROWS, tm), lambda bi, i: (bi, 0, i)),
        ],
        out_shape=[
            jax.ShapeDtypeStruct((b, t, D_MODEL), F32),
            jax.ShapeDtypeStruct((n_total, D_MODEL), F32),
            jax.ShapeDtypeStruct((b, ROUTE_ROWS, t), F32),
        ],
        input_output_aliases=aliases,
        compiler_params=_params(("arbitrary", "arbitrary")),
        name="post_attention",
    )(*args)


def _dispatch(route_rows, n_tokens, n_tiles):
    tile = MOE_TILE
    n_pad = n_tiles * tile - n_tokens
    classes = jnp.arange(N_CLASSES, dtype=jnp.int32)
    cls = route_rows[:, 0].astype(jnp.int32)
    counts = jnp.sum((cls[:, None] == classes[None, :]).astype(jnp.int32), axis=0)
    need_end = jnp.cumsum((-counts) % tile)
    pad_ids = jnp.arange(n_pad, dtype=jnp.int32)
    pad_cls = jnp.sum((need_end[None, :] <= pad_ids[:, None]).astype(jnp.int32), axis=1)
    cls_s, tok_s, wlo_s, whi_s = lax.sort(
        (jnp.concatenate([cls, pad_cls]),
         jnp.concatenate([jnp.arange(n_tokens, dtype=jnp.int32), jnp.full((n_pad,), -1, jnp.int32)]),
         jnp.concatenate([route_rows[:, 1], jnp.zeros((n_pad,), F32)]),
         jnp.concatenate([route_rows[:, 2], jnp.zeros((n_pad,), F32)])),
        num_keys=1, is_stable=True)
    tile_cls = jnp.minimum(cls_s.reshape(n_tiles, tile)[:, 0], N_CLASSES - 1)
    grp, pair = tile_cls // len(PAIRS), tile_cls % len(PAIRS)
    pair_i = sum(jnp.where(pair == n, i, 0) for n, (i, _) in enumerate(PAIRS))
    pair_j = sum(jnp.where(pair == n, j, 0) for n, (_, j) in enumerate(PAIRS))
    ea = grp * EXPERTS_PER_GROUP + pair_i
    eb = grp * EXPERTS_PER_GROUP + pair_j
    tok_s = tok_s.reshape(n_tiles, tile)
    valid = tok_s >= 0
    src = jnp.where(valid, tok_s, 0)
    t_ids = jnp.arange(n_tiles, dtype=jnp.int32)[:, None]
    r_ids = jnp.arange(tile, dtype=jnp.int32)[None, :]
    dst = jnp.where(valid, tok_s, n_tokens + (t_ids % 2) * tile + r_ids)
    dst_prev = jnp.concatenate([n_tokens + tile + r_ids, dst[:-1]], axis=0)
    w = jnp.stack([wlo_s.reshape(n_tiles, tile), whi_s.reshape(n_tiles, tile)], axis=1)
    shape = (n_tiles, 1, tile)
    return ea, eb, src.reshape(shape), dst.reshape(shape), dst_prev.reshape(shape), w


N_DMA_CHUNKS = 8


def _experts_kernel(ea_ref, eb_ref, src_ref, srcn_ref, dst_ref, dstp_ref, w_ref, h_hbm,
                    wga_ref, wua_ref, wda_ref, wgb_ref, wub_ref, wdb_ref, y_in_hbm, y_hbm,
                    xbuf, ybuf, xb_ref, ha_ref, gsem, ssem):
    del eb_ref, y_in_hbm
    t = pl.program_id(0)
    n_t = pl.num_programs(0)
    tile = xbuf.shape[1]
    slot = t % 2
    other = 1 - slot
    chunk = tile // N_DMA_CHUNKS

    def gather_rows(idx_ref, s, rows):
        for r in rows:
            pltpu.make_async_copy(h_hbm.at[pl.ds(idx_ref[0, 0, r], 1), :],
                                  xbuf.at[s, pl.ds(r, 1), :], gsem.at[s]).start()

    def wait_gather(s):
        pltpu.make_async_copy(h_hbm.at[pl.ds(0, tile), :], xbuf.at[s], gsem.at[s]).wait()

    def scatter_rows(idx_ref, s, rows):
        for r in rows:
            pltpu.make_async_copy(ybuf.at[s, pl.ds(r, 1), :],
                                  y_hbm.at[pl.ds(idx_ref[0, 0, r], 1), :], ssem.at[s]
                                  ).start(priority=1)

    def wait_scatter(s):
        pltpu.make_async_copy(ybuf.at[s], y_hbm.at[pl.ds(0, tile), :], ssem.at[s]).wait()

    @pl.when(t == 0)
    def _():
        ybuf[...] = jnp.zeros(ybuf.shape, F32)
        gather_rows(src_ref, 0, range(tile))

    wait_gather(slot)
    xb_ref[...] = xbuf[slot].astype(BF16)
    x = xb_ref[...]
    wrow = w_ref[0]

    def issue(c):
        rows = range(c * chunk, (c + 1) * chunk)
        scatter_rows(dstp_ref, other, rows)
        gather_rows(srcn_ref, other, rows)

    def column(row):
        c = jnp.broadcast_to(row, (V7X_LANES, tile)).T
        return jnp.concatenate([c] * (D_EXPERT // V7X_LANES), axis=1)

    def hidden(wg_ref, wu_ref, gate, c0):
        issue(c0)
        a = jnp.dot(x, wg_ref[0, 0], preferred_element_type=F32)
        issue(c0 + 1)
        u = jnp.dot(x, wu_ref[0, 0], preferred_element_type=F32)
        return ((a * jax.nn.sigmoid(a)) * u * gate).astype(BF16)

    always = ea_ref[t] >= 0

    @pl.when(always)
    def _():
        issue(0)
        issue(1)
        ha_ref[...] = hidden(wga_ref, wua_ref, column(wrow[0:1]), 2)

    @pl.when(t >= 1)
    def _():
        wait_scatter(slot)

    @pl.when(always)
    def _():
        hb = hidden(wgb_ref, wub_ref, column(wrow[1:2]), 4)
        issue(6)
        y = jnp.dot(ha_ref[...], wda_ref[0, 0], preferred_element_type=F32)
        issue(7)
        y = y + jnp.dot(hb, wdb_ref[0, 0], preferred_element_type=F32)
        ybuf[slot] = y

    @pl.when(t == n_t - 1)
    def _():
        scatter_rows(dst_ref, slot, range(tile))
        wait_scatter(other)
        wait_scatter(slot)
        wait_gather(other)


def _experts(h2_all, y_buf, disp, layer, wg, wu, wd):
    ea, eb, src, dst, dst_prev, w = disp
    n_tokens = h2_all.shape[0]
    n_tiles = src.shape[0]
    tile = MOE_TILE
    idx_spec = lambda fn: pl.BlockSpec((1, 1, tile), fn, memory_space=pltpu.SMEM)
    wspec = lambda shape, sel: pl.BlockSpec((1, 1) + shape, sel)
    grid_spec = pltpu.PrefetchScalarGridSpec(
        num_scalar_prefetch=2,
        grid=(n_tiles,),
        in_specs=[
            idx_spec(lambda t, ea, eb: (t, 0, 0)),
            idx_spec(lambda t, ea, eb: (jnp.minimum(t + 1, n_tiles - 1), 0, 0)),
            idx_spec(lambda t, ea, eb: (t, 0, 0)),
            idx_spec(lambda t, ea, eb: (t, 0, 0)),
            pl.BlockSpec((1, 2, tile), lambda t, ea, eb: (t, 0, 0)),
            pl.BlockSpec(memory_space=pl.ANY),
            wspec((D_MODEL, D_EXPERT), lambda t, ea, eb: (layer, ea[t], 0, 0)),
            wspec((D_MODEL, D_EXPERT), lambda t, ea, eb: (layer, ea[t], 0, 0)),
            wspec((D_EXPERT, D_MODEL), lambda t, ea, eb: (layer, ea[t], 0, 0)),
            wspec((D_MODEL, D_EXPERT), lambda t, ea, eb: (layer, eb[t], 0, 0)),
            wspec((D_MODEL, D_EXPERT), lambda t, ea, eb: (layer, eb[t], 0, 0)),
            wspec((D_EXPERT, D_MODEL), lambda t, ea, eb: (layer, eb[t], 0, 0)),
            pl.BlockSpec(memory_space=pl.ANY),
        ],
        out_specs=pl.BlockSpec(memory_space=pl.ANY),
        scratch_shapes=[
            pltpu.VMEM((2, tile, D_MODEL), F32),
            pltpu.VMEM((2, tile, D_MODEL), F32),
            pltpu.VMEM((tile, D_MODEL), BF16),
            pltpu.VMEM((tile, D_EXPERT), BF16),
            pltpu.SemaphoreType.DMA((2,)),
            pltpu.SemaphoreType.DMA((2,)),
        ],
    )
    return pl.pallas_call(
        _experts_kernel,
        grid_spec=grid_spec,
        out_shape=jax.ShapeDtypeStruct((n_tokens + 2 * tile, D_MODEL), F32),
        input_output_aliases={14: 0},
        compiler_params=_params(("arbitrary",)),
        name="experts",
    )(ea, eb, src, src, dst, dst_prev, w, h2_all, wg, wu, wd, wg, wu, wd, y_buf)


def _rope_tables(t):
    half = HEAD_DIM // 2
    inv_freq = ROPE_THETA ** (-jnp.arange(0, half, 2, dtype=F32) / half)
    pos = jnp.arange(t, dtype=jnp.int32)
    ang_r = (pos // GRID_W).astype(F32)[None, :] * inv_freq[:, None]
    ang_c = (pos % GRID_W).astype(F32)[None, :] * inv_freq[:, None]
    return jnp.cos(ang_r), jnp.sin(ang_r), jnp.cos(ang_c), jnp.sin(ang_c)


def kernel(x_prompt, x_sample, cache_k, cache_v, c, c_ctx, norm1_g, norm2_g, ada_w, ada_b, w_in,
           q_norm_g, k_norm_g, pool_w, pool_scale, w_out, router_w, router_bias, expert_w_gate,
           expert_w_up, expert_w_down):
    n_ctx, t_ctx, _ = x_prompt.shape
    n_lat, t_lat, _ = x_sample.shape
    past = cache_k.shape[2]
    tok_ctx = n_ctx * t_ctx
    n_tokens = tok_ctx + n_lat * t_lat
    n_tiles = n_tokens // MOE_TILE + N_CLASSES
    tm_lat = 512

    cond = jnp.concatenate([c_ctx[None, :], c,
                            jnp.zeros((MOD_ROWS - 1 - n_lat, D_MODEL), F32)], axis=0)
    mod = _modulation(cond, ada_w, ada_b).reshape(DEPTH, MOD_ROWS, 1, 6 * D_MODEL)
    rope_tabs = _rope_tables(t_lat)

    wrT = router_w.T
    wr_hi = wrT.astype(BF16)
    wr2 = jnp.concatenate([wr_hi, (wrT - wr_hi.astype(F32)).astype(BF16)], axis=0)
    rb = router_bias.reshape(N_EXPERTS, 1)
    wg, wu, wd = _to_bf16(expert_w_gate), _to_bf16(expert_w_up), _to_bf16(expert_w_down)

    xp, xs = x_prompt, x_sample
    h2_all = jnp.zeros((n_tokens, D_MODEL), F32)
    y_all = jnp.zeros((n_tokens + 2 * MOE_TILE, D_MODEL), F32)
    new_k, new_v = [], []
    for l in range(DEPTH):
        w_in_b = w_in[l].astype(BF16)
        wqkvT = w_in_b[:, :QKV_WIDTH].T
        wp = w_in_b[:, QKV_WIDTH:]
        wo = w_out[l].astype(BF16)
        wo_top, wo_bot = wo[:ATTN_WIDTH], wo[ATTN_WIDTH:]
        pw = pool_w[l].astype(BF16)
        ps = pool_scale[l].reshape(1, POOL_WIDTH)
        g1 = norm1_g[l].reshape(1, D_MODEL)
        g2 = norm2_g[l].reshape(1, D_MODEL)
        gq = q_norm_g[l].reshape(HEAD_DIM, 1)
        gk = k_norm_g[l].reshape(HEAD_DIM, 1)
        mod_l = mod[l]
        res_ctx = None if l == 0 else (y_all, 0, mod[l - 1])
        res_lat = None if l == 0 else (y_all, tok_ctx, mod[l - 1])

        outs = _pre(xp, res_ctx, mod_l, 0, False, g1, wqkvT, wp, gq, gk, None, t_ctx, True)
        if l == 0:
            qT, k, vT, p, k32, v32 = outs
        else:
            qT, k, vT, p, xp, k32, v32 = outs
        aT = _attention(qT, k, vT, t_ctx, t_ctx, 1)
        xp, h2_all, route_ctx = _post(xp, aT, p, h2_all, 0, mod_l, 0, False, wo_top, wo_bot,
                                      pw, ps, g2, wr2, rb, t_ctx)
        new_k.append(k32.reshape(n_ctx, t_ctx, N_KV_HEADS, HEAD_DIM))
        new_v.append(v32.reshape(n_ctx, t_ctx, N_KV_HEADS, HEAD_DIM))

        outs = _pre(xs, res_lat, mod_l, 1, True, g1, wqkvT, wp, gq, gk, rope_tabs, tm_lat, False)
        if l == 0:
            qT, k, vT, p = outs
        else:
            qT, k, vT, p, xs = outs
        ck = cache_k[:, l].reshape(n_lat, past, KV_WIDTH).astype(QK8)
        cvT = cache_v[:, l].reshape(n_lat, past, KV_WIDTH).astype(QK8).transpose(0, 2, 1)
        k_all = jnp.concatenate([ck, k], axis=1)
        vT_all = jnp.concatenate([cvT, vT], axis=2)
        aT = _attention(qT, k_all, vT_all, 256, 512, 3)
        xs, h2_all, route_lat = _post(xs, aT, p, h2_all, tok_ctx, mod_l, 1, True, wo_top,
                                      wo_bot, pw, ps, g2, wr2, rb, 256)

        route_rows = jnp.concatenate(
            [route_ctx.transpose(0, 2, 1).reshape(tok_ctx, ROUTE_ROWS),
             route_lat.transpose(0, 2, 1).reshape(n_tokens - tok_ctx, ROUTE_ROWS)], axis=0)
        y_all = _experts(h2_all, y_all, _dispatch(route_rows, n_tokens, n_tiles), l, wg, wu, wd)

    mod_l = mod[DEPTH - 1]
    xp = _finish(xp, y_all, 0, mod_l, 0, False, t_ctx)
    xs = _finish(xs, y_all, tok_ctx, mod_l, 1, True, tm_lat)
    return xp, xs, jnp.stack(new_k, axis=1), jnp.stack(new_v, axis=1)
```

```python
import functools

import jax
import jax.numpy as jnp
from jax import lax
from jax.experimental import pallas as pl
from jax.experimental.pallas import tpu as pltpu

D_MODEL = 1024
DEPTH = 4
GRID_W = 64
HEAD_DIM = 64
N_Q_HEADS = 8
N_KV_HEADS = 2
Q_PER_KV = N_Q_HEADS // N_KV_HEADS
ATTN_WIDTH = N_Q_HEADS * HEAD_DIM
KV_WIDTH = N_KV_HEADS * HEAD_DIM
QKV_WIDTH = ATTN_WIDTH + 2 * KV_WIDTH
POOL_WINDOWS = (2, 4, 8, 16)
POOL_WIDTH = D_MODEL - ATTN_WIDTH
POOL_GROUP_WIDTH = POOL_WIDTH // len(POOL_WINDOWS)
N_EXPERTS = 16
N_EXPERT_GROUPS = 4
EXPERTS_PER_GROUP = N_EXPERTS // N_EXPERT_GROUPS
D_EXPERT = 512
ROPE_THETA = 10000.0
EPS = 1e-6

Q_SCALE = HEAD_DIM ** -0.5 * 1.4426950408889634
DENOM_ROWS = 32
P_SHIFT = 8.0
MOD_ROWS = 16
POOL_HALO = 16
V7X_LANES = 128
VMEM_LIMIT = 48 * 1024 * 1024

PAIRS = tuple((i, j) for i in range(EXPERTS_PER_GROUP) for j in range(i + 1, EXPERTS_PER_GROUP))
N_CLASSES = N_EXPERT_GROUPS * len(PAIRS)
ROUTE_ROWS = 8
MOE_TILE = 256

F32 = jnp.float32
BF16 = jnp.bfloat16
QK8 = jnp.float8_e4m3fn
NT_DIMS = (((1,), (1,)), ((), ()))
TN_DIMS = (((0,), (0,)), ((), ()))


def _params(sem):
    return pltpu.CompilerParams(dimension_semantics=sem, vmem_limit_bytes=VMEM_LIMIT)


def _mod_kernel(c_ref, w_ref, b_ref, o_ref):
    c = c_ref[...]
    a = c * jax.nn.sigmoid(c)
    o_ref[0] = jnp.dot(a, w_ref[0], preferred_element_type=F32,
                       precision=lax.Precision.HIGHEST) + b_ref[0]


def _modulation(cond, ada_w, ada_b):
    tn = 1536
    n_out = ada_w.shape[2]
    return pl.pallas_call(
        _mod_kernel,
        grid=(DEPTH, n_out // tn),
        in_specs=[
            pl.BlockSpec((MOD_ROWS, D_MODEL), lambda l, n: (0, 0)),
            pl.BlockSpec((1, D_MODEL, tn), lambda l, n: (l, 0, n)),
            pl.BlockSpec((1, 1, tn), lambda l, n: (l, 0, n)),
        ],
        out_specs=pl.BlockSpec((1, MOD_ROWS, tn), lambda l, n: (l, 0, n)),
        out_shape=jax.ShapeDtypeStruct((DEPTH, MOD_ROWS, n_out), F32),
        compiler_params=_params(("arbitrary", "arbitrary")),
        name="modulation",
    )(cond, ada_w, ada_b.reshape(DEPTH, 1, n_out))


def _cast_kernel(x_ref, o_ref):
    o_ref[...] = x_ref[...].astype(o_ref.dtype)


def _to_bf16(w):
    blk = pl.BlockSpec((1, 1) + w.shape[2:], lambda l, e: (l, e, 0, 0))
    return pl.pallas_call(
        _cast_kernel, grid=w.shape[:2], in_specs=[blk], out_specs=blk,
        out_shape=jax.ShapeDtypeStruct(w.shape, BF16),
        compiler_params=_params(("parallel", "parallel")), name="cast_bf16")(w)


def _rms_modulate(x, g, shift, scale):
    ms = jnp.mean(x * x, axis=-1, keepdims=True)
    h = x * lax.rsqrt(ms + EPS) * g
    return h * (1.0 + scale) + shift


def _pre_kernel(*refs, rope, emit_f32, residual):
    refs = list(refs)
    x_ref = refs.pop(0)
    if residual:
        y_ref, modp_ref = refs.pop(0), refs.pop(0)
    mod_ref, g1_ref, wqkv_ref, wp_ref, gq_ref, gk_ref = refs[:6]
    refs = refs[6:]
    if rope:
        cr_ref, sr_ref, cc_ref, sc_ref = refs[:4]
        refs = refs[4:]
    qT_ref, k_ref, vT_ref, p_ref = refs[:4]
    refs = refs[4:]
    tm = x_ref.shape[1]
    x = x_ref[0]
    if residual:
        x = x + modp_ref[0][:, 5 * D_MODEL:6 * D_MODEL] * y_ref[...]
        refs.pop(0)[0] = x
    m = mod_ref[0]
    h = _rms_modulate(x, g1_ref[...], m[:, 0:D_MODEL], m[:, D_MODEL:2 * D_MODEL])
    hb = h.astype(BF16)
    zT = lax.dot_general(wqkv_ref[...], hb, NT_DIMS, preferred_element_type=F32)
    p_ref[0] = jnp.dot(hb, wp_ref[...], preferred_element_type=F32).astype(BF16)

    def norm_rope(zt, n_heads, g_ref):
        z3 = zt.reshape(n_heads, HEAD_DIM, tm)
        ms = jnp.mean(z3 * z3, axis=1, keepdims=True)
        y = z3 * lax.rsqrt(ms + EPS) * g_ref[...][None]
        if rope:
            cr, sr = cr_ref[...][None], sr_ref[...][None]
            cc, sc = cc_ref[...][None], sc_ref[...][None]
            x1r, x2r, x1c, x2c = y[:, 0:16], y[:, 16:32], y[:, 32:48], y[:, 48:64]
            y = jnp.concatenate([x1r * cr - x2r * sr, x1r * sr + x2r * cr,
                                 x1c * cc - x2c * sc, x1c * sc + x2c * cc], axis=1)
        return y.reshape(n_heads * HEAD_DIM, tm)

    qT = norm_rope(zT[0:ATTN_WIDTH], N_Q_HEADS, gq_ref)
    qT_ref[0] = (qT * Q_SCALE).astype(BF16)
    kT = norm_rope(zT[ATTN_WIDTH:ATTN_WIDTH + KV_WIDTH], N_KV_HEADS, gk_ref)
    k_tok = kT.T
    k_ref[0] = k_tok.astype(QK8)
    vT = zT[ATTN_WIDTH + KV_WIDTH:QKV_WIDTH]
    vT_ref[0] = vT.astype(QK8)
    if emit_f32:
        k32_ref, v32_ref = refs[:2]
        k32_ref[0] = k_tok
        v32_ref[0] = vT.T


def _pre(x, res, mod_l, row0, per_batch_mod, g1, wqkvT, wp, gq, gk, rope_tabs, tm, emit_f32):
    b, t, _ = x.shape
    rope = rope_tabs is not None
    mod_map = (lambda bi, i: (row0 + bi, 0, 0)) if per_batch_mod else (lambda bi, i: (row0, 0, 0))
    tok_spec = pl.BlockSpec((1, tm, D_MODEL), lambda bi, i: (bi, i, 0))
    in_specs, args = [tok_spec], [x]
    if res is not None:
        y_all, row_off, mod_prev = res
        blk0, per_b = row_off // tm, t // tm
        in_specs += [pl.BlockSpec((tm, D_MODEL), lambda bi, i: (blk0 + bi * per_b + i, 0)),
                     pl.BlockSpec((1, 1, 6 * D_MODEL), mod_map)]
        args += [y_all, mod_prev]
    in_specs += [
        pl.BlockSpec((1, 1, 6 * D_MODEL), mod_map),
        pl.BlockSpec((1, D_MODEL), lambda bi, i: (0, 0)),
        pl.BlockSpec((QKV_WIDTH, D_MODEL), lambda bi, i: (0, 0)),
        pl.BlockSpec((D_MODEL, POOL_WIDTH), lambda bi, i: (0, 0)),
        pl.BlockSpec((HEAD_DIM, 1), lambda bi, i: (0, 0)),
        pl.BlockSpec((HEAD_DIM, 1), lambda bi, i: (0, 0)),
    ]
    args += [mod_l, g1, wqkvT, wp, gq, gk]
    if rope:
        in_specs += [pl.BlockSpec((16, tm), lambda bi, i: (0, i))] * 4
        args += list(rope_tabs)
    out_shape = [
        jax.ShapeDtypeStruct((b, ATTN_WIDTH, t), BF16),
        jax.ShapeDtypeStruct((b, t, KV_WIDTH), QK8),
        jax.ShapeDtypeStruct((b, KV_WIDTH, t), QK8),
        jax.ShapeDtypeStruct((b, t, POOL_WIDTH), BF16),
    ]
    out_specs = [
        pl.BlockSpec((1, ATTN_WIDTH, tm), lambda bi, i: (bi, 0, i)),
        pl.BlockSpec((1, tm, KV_WIDTH), lambda bi, i: (bi, i, 0)),
        pl.BlockSpec((1, KV_WIDTH, tm), lambda bi, i: (bi, 0, i)),
        pl.BlockSpec((1, tm, POOL_WIDTH), lambda bi, i: (bi, i, 0)),
    ]
    if res is not None:
        out_shape.append(jax.ShapeDtypeStruct((b, t, D_MODEL), F32))
        out_specs.append(tok_spec)
    if emit_f32:
        out_shape += [jax.ShapeDtypeStruct((b, t, KV_WIDTH), F32)] * 2
        out_specs += [pl.BlockSpec((1, tm, KV_WIDTH), lambda bi, i: (bi, i, 0))] * 2
    return pl.pallas_call(
        functools.partial(_pre_kernel, rope=rope, emit_f32=emit_f32, residual=res is not None),
        grid=(b, t // tm),
        in_specs=in_specs,
        out_specs=out_specs,
        out_shape=out_shape,
        compiler_params=_params(("parallel", "parallel")),
        name="pre_attention",
    )(*args)


def _finish_kernel(x_ref, y_ref, mod_ref, o_ref):
    o_ref[0] = x_ref[0] + mod_ref[0][:, 5 * D_MODEL:6 * D_MODEL] * y_ref[...]


def _finish(xmid, y_all, row_off, mod_l, row0, per_batch_mod, tm):
    b, t, _ = xmid.shape
    blk0, per_b = row_off // tm, t // tm
    mod_map = (lambda bi, i: (row0 + bi, 0, 0)) if per_batch_mod else (lambda bi, i: (row0, 0, 0))
    tok_spec = pl.BlockSpec((1, tm, D_MODEL), lambda bi, i: (bi, i, 0))
    return pl.pallas_call(
        _finish_kernel,
        grid=(b, t // tm),
        in_specs=[tok_spec,
                  pl.BlockSpec((tm, D_MODEL), lambda bi, i: (blk0 + bi * per_b + i, 0)),
                  pl.BlockSpec((1, 1, 6 * D_MODEL), mod_map)],
        out_specs=tok_spec,
        out_shape=jax.ShapeDtypeStruct((b, t, D_MODEL), F32),
        compiler_params=_params(("parallel", "parallel")),
        name="moe_residual",
    )(xmid, y_all, mod_l)


def _attn_kernel(qT_ref, k_ref, vT_ref, o_ref, qpad_ref, acc_ref, m_ref, *, tk, unroll):
    tq = qT_ref.shape[2]
    s_len = k_ref.shape[1]
    n_pairs = N_Q_HEADS // 2
    zeros = jnp.zeros((HEAD_DIM, tq), QK8)
    for h in range(N_Q_HEADS):
        qh = qT_ref[0, h * HEAD_DIM:(h + 1) * HEAD_DIM, :].astype(QK8)
        qpad_ref[h // 2, :, (h % 2) * tq:(h % 2 + 1) * tq] = jnp.concatenate(
            [qh, zeros] if h < Q_PER_KV else [zeros, qh], axis=0)
    m_ref[...] = jnp.full(m_ref.shape, -jnp.inf, F32)
    acc_ref[...] = jnp.zeros(acc_ref.shape, F32)

    n_items = unroll * n_pairs

    def step(t, carry):
        ones = jnp.ones((DENOM_ROWS, tk), QK8)

        def keys(u):
            off = pl.multiple_of((t * unroll + u) * tk, tk)
            return k_ref[0, pl.ds(off, tk), :]

        def values(u, j):
            off = pl.multiple_of((t * unroll + u) * tk, tk)
            vb = vT_ref[0, j * HEAD_DIM:(j + 1) * HEAD_DIM, pl.ds(off, tk)]
            return jnp.concatenate([vb, ones], axis=0)

        def scores(i):
            return jnp.dot(keys(i // n_pairs), qpad_ref[i % n_pairs],
                           preferred_element_type=F32).astype(BF16)

        queue = [scores(0), scores(1)]
        for i in range(n_items):
            u, pr = divmod(i, n_pairs)
            j = (2 * pr) // Q_PER_KV
            s = queue.pop(0)
            m_old = m_ref[pr]
            m_new = jnp.maximum(m_old, jnp.max(s, axis=0, keepdims=True).astype(F32))
            c_old = (m_old - P_SHIFT).astype(BF16)
            c_new = (m_new - P_SHIFT).astype(BF16)
            alpha = jnp.exp2(c_old.astype(F32) - c_new.astype(F32))
            p = jnp.exp2(jnp.minimum(s - c_new, P_SHIFT)).astype(QK8)
            if i + 2 < n_items:
                queue.append(scores(i + 2))
            pv = jnp.dot(values(u, j), p, preferred_element_type=F32)
            acc_ref[pr] = alpha * acc_ref[pr] + pv
            m_ref[pr] = m_new
        return carry

    lax.fori_loop(0, s_len // (tk * unroll), step, 0)
    for h in range(N_Q_HEADS):
        lanes = slice((h % 2) * tq, (h % 2 + 1) * tq)
        o_ref[0, h * HEAD_DIM:(h + 1) * HEAD_DIM, :] = (
            acc_ref[h // 2, 0:HEAD_DIM, lanes]
            / acc_ref[h // 2, HEAD_DIM:HEAD_DIM + 1, lanes]).astype(BF16)


def _attention(qT, k_all, vT_all, tq, tk, unroll):
    b, _, t = qT.shape
    s_len = k_all.shape[1]
    return pl.pallas_call(
        functools.partial(_attn_kernel, tk=tk, unroll=unroll),
        grid=(b, t // tq),
        in_specs=[
            pl.BlockSpec((1, ATTN_WIDTH, tq), lambda bi, i: (bi, 0, i)),
            pl.BlockSpec((1, s_len, KV_WIDTH), lambda bi, i: (bi, 0, 0)),
            pl.BlockSpec((1, KV_WIDTH, s_len), lambda bi, i: (bi, 0, 0)),
        ],
        out_specs=pl.BlockSpec((1, ATTN_WIDTH, tq), lambda bi, i: (bi, 0, i)),
        out_shape=jax.ShapeDtypeStruct((b, ATTN_WIDTH, t), BF16),
        scratch_shapes=[
            pltpu.VMEM((N_Q_HEADS // 2, 2 * HEAD_DIM, 2 * tq), QK8),
            pltpu.VMEM((N_Q_HEADS // 2, HEAD_DIM + DENOM_ROWS, 2 * tq), F32),
            pltpu.VMEM((N_Q_HEADS // 2, 1, 2 * tq), F32),
        ],
        compiler_params=_params(("parallel", "parallel")),
        name="attention",
    )(qT, k_all, vT_all)


def _route(logits, bias):
    aff = jax.nn.sigmoid(logits)
    sel = aff + bias
    eidx = lax.broadcasted_iota(jnp.int32, sel.shape, 0).astype(F32)
    scores = []
    for g in range(N_EXPERT_GROUPS):
        r = [sel[EXPERTS_PER_GROUP * g + i:EXPERTS_PER_GROUP * g + i + 1, :]
             for i in range(EXPERTS_PER_GROUP)]
        best = None
        for i, j in PAIRS:
            sij = r[i] + r[j]
            best = sij if best is None else jnp.maximum(best, sij)
        scores.append(best)
    gmax = functools.reduce(jnp.maximum, scores)
    gstar = jnp.where(scores[0] == gmax, 0.0,
                      jnp.where(scores[1] == gmax, 1.0, jnp.where(scores[2] == gmax, 2.0, 3.0)))
    in_grp = jnp.floor(eidx * (1.0 / EXPERTS_PER_GROUP)) == gstar
    cand = jnp.where(in_grp, sel, -jnp.inf)
    m1 = jnp.max(cand, axis=0, keepdims=True)
    idx1 = jnp.min(jnp.where(cand == m1, eidx, float(N_EXPERTS)), axis=0, keepdims=True)
    cand2 = jnp.where(eidx == idx1, -jnp.inf, cand)
    m2 = jnp.max(cand2, axis=0, keepdims=True)
    idx2 = jnp.min(jnp.where(cand2 == m2, eidx, float(N_EXPERTS)), axis=0, keepdims=True)
    lo, hi = jnp.minimum(idx1, idx2), jnp.maximum(idx1, idx2)
    a_lo = jnp.sum(jnp.where(eidx == lo, aff, 0.0), axis=0, keepdims=True)
    a_hi = jnp.sum(jnp.where(eidx == hi, aff, 0.0), axis=0, keepdims=True)
    den = a_lo + a_hi
    i_in = lo - EXPERTS_PER_GROUP * gstar
    j_in = hi - EXPERTS_PER_GROUP * gstar
    first = jnp.where(i_in == 0.0, 0.0, jnp.where(i_in == 1.0, 2.0, 3.0))
    cls = gstar * len(PAIRS) + first + j_in - 1.0
    pad = jnp.zeros((ROUTE_ROWS - 3, logits.shape[1]), F32)
    return jnp.concatenate([cls, a_lo / den, a_hi / den, pad], axis=0)


def _post_kernel(x_ref, aT_ref, p_ref, pprev_ref, pnext_ref, mod_ref, wot_ref, wob_ref, pw_ref,
                 ps_ref, g2_ref, wr_ref, rb_ref, h2_in_ref, xmid_ref, h2_ref, route_ref, *, seq_len):
    del h2_in_ref
    tm = x_ref.shape[1]
    i = pl.program_id(1)
    n_tiles = seq_len // tm
    m = mod_ref[0]
    gate_msa = m[:, 2 * D_MODEL:3 * D_MODEL]
    shift2 = m[:, 3 * D_MODEL:4 * D_MODEL]
    scale2 = m[:, 4 * D_MODEL:5 * D_MODEL]

    pc = p_ref[0].astype(F32)
    prev = jnp.where(i > 0, pprev_ref[0].astype(F32), 0.0)
    nxt = jnp.where(i < n_tiles - 1, pnext_ref[0].astype(F32), 0.0)
    ext = jnp.concatenate([prev, pc, nxt], axis=0)
    t_seq = lax.broadcasted_iota(jnp.int32, (tm, POOL_GROUP_WIDTH), 0) + i * tm
    pools = []
    for g, w in enumerate(POOL_WINDOWS):
        lanes = slice(g * POOL_GROUP_WIDTH, (g + 1) * POOL_GROUP_WIDTH)
        a = ext[:, lanes]
        span = 1
        while span < w:
            a = a[:-span] + a[span:]
            span *= 2
        start = POOL_HALO - w // 2
        wsum = a[start:start + tm]
        lo = jnp.maximum(t_seq - w // 2, 0)
        hi = jnp.minimum(t_seq + w // 2, seq_len)
        pooled = wsum / (hi - lo).astype(F32) - pc[:, lanes]
        mixed = jnp.dot(pooled.astype(BF16), pw_ref[g], preferred_element_type=F32)
        pools.append(mixed * ps_ref[:, lanes])
    pool = jnp.concatenate(pools, axis=1).astype(BF16)

    proj = lax.dot_general(aT_ref[0], wot_ref[...], TN_DIMS, preferred_element_type=F32)
    proj = proj + jnp.dot(pool, wob_ref[...], preferred_element_type=F32)
    xm = x_ref[0] + gate_msa * proj
    xmid_ref[0] = xm

    h2 = _rms_modulate(xm, g2_ref[...], shift2, scale2)
    h2_ref[...] = h2
    h_hi = h2.astype(BF16)
    h_lo = (h2 - h_hi.astype(F32)).astype(BF16)
    r = lax.dot_general(wr_ref[...], jnp.concatenate([h_hi, h_lo], axis=0), NT_DIMS,
                        preferred_element_type=F32)
    logits = r[0:N_EXPERTS, 0:tm] + r[0:N_EXPERTS, tm:2 * tm] + r[N_EXPERTS:2 * N_EXPERTS, 0:tm]
    route_ref[0] = _route(logits, rb_ref[...])


def _post(x, aT, p, h2_buf, row_off, mod_l, row0, per_batch_mod, wo_top, wo_bot, pool_w,
          pool_scale, g2, wr2, rb, tm):
    b, t, _ = x.shape
    n_total = h2_buf.shape[0]
    hb = tm // POOL_HALO
    last = t // POOL_HALO - 1
    blk0, per_b = row_off // tm, t // tm
    mod_map = (lambda bi, i: (row0 + bi, 0, 0)) if per_batch_mod else (lambda bi, i: (row0, 0, 0))
    const2 = lambda bi, i: (0, 0)
    tok_spec = pl.BlockSpec((1, tm, D_MODEL), lambda bi, i: (bi, i, 0))
    in_specs = [
        tok_spec,
        pl.BlockSpec((1, ATTN_WIDTH, tm), lambda bi, i: (bi, 0, i)),
        pl.BlockSpec((1, tm, POOL_WIDTH), lambda bi, i: (bi, i, 0)),
        pl.BlockSpec((1, POOL_HALO, POOL_WIDTH), lambda bi, i: (bi, jnp.maximum(i * hb - 1, 0), 0)),
        pl.BlockSpec((1, POOL_HALO, POOL_WIDTH), lambda bi, i: (bi, jnp.minimum((i + 1) * hb, last), 0)),
        pl.BlockSpec((1, 1, 6 * D_MODEL), mod_map),
        pl.BlockSpec((ATTN_WIDTH, D_MODEL), const2),
        pl.BlockSpec((POOL_WIDTH, D_MODEL), const2),
        pl.BlockSpec((len(POOL_WINDOWS), POOL_GROUP_WIDTH, POOL_GROUP_WIDTH), lambda bi, i: (0, 0, 0)),
        pl.BlockSpec((1, POOL_WIDTH), const2),
        pl.BlockSpec((1, D_MODEL), const2),
        pl.BlockSpec((2 * N_EXPERTS, D_MODEL), const2),
        pl.BlockSpec((N_EXPERTS, 1), const2),
        pl.BlockSpec(memory_space=pl.ANY),
    ]
    args = [x, aT, p, p, p, mod_l, wo_top, wo_bot, pool_w, pool_scale, g2, wr2, rb, h2_buf]
    aliases = {len(args) - 1: 1}
    return pl.pallas_call(
        functools.partial(_post_kernel, seq_len=t),
        grid=(b, t // tm),
        in_specs=in_specs,
        out_specs=[
            tok_spec,
            pl.BlockSpec((tm, D_MODEL), lambda bi, i: (blk0 + bi * per_b + i, 0)),
            pl.BlockSpec((1, ROUTE_ROWS, tm), lambda bi, i: (bi, 0, i)),
        ],
        out_shape=[
            jax.ShapeDtypeStruct((b, t, D_MODEL), F32),
            jax.ShapeDtypeStruct((n_total, D_MODEL), F32),
            jax.ShapeDtypeStruct((b, ROUTE_ROWS, t), F32),
        ],
        input_output_aliases=aliases,
        compiler_params=_params(("arbitrary", "arbitrary")),
        name="post_attention",
    )(*args)


def _dispatch(route_rows, n_tokens, n_tiles):
    tile = MOE_TILE
    n_pad = n_tiles * tile - n_tokens
    classes = jnp.arange(N_CLASSES, dtype=jnp.int32)
    cls = route_rows[:, 0].astype(jnp.int32)
    counts = jnp.sum((cls[:, None] == classes[None, :]).astype(jnp.int32), axis=0)
    need_end = jnp.cumsum((-counts) % tile)
    pad_ids = jnp.arange(n_pad, dtype=jnp.int32)
    pad_cls = jnp.sum((need_end[None, :] <= pad_ids[:, None]).astype(jnp.int32), axis=1)
    cls_s, tok_s, wlo_s, whi_s = lax.sort(
        (jnp.concatenate([cls, pad_cls]),
         jnp.concatenate([jnp.arange(n_tokens, dtype=jnp.int32), jnp.full((n_pad,), -1, jnp.int32)]),
         jnp.concatenate([route_rows[:, 1], jnp.zeros((n_pad,), F32)]),
         jnp.concatenate([route_rows[:, 2], jnp.zeros((n_pad,), F32)])),
        num_keys=1, is_stable=True)
    tile_cls = jnp.minimum(cls_s.reshape(n_tiles, tile)[:, 0], N_CLASSES - 1)
    grp, pair = tile_cls // len(PAIRS), tile_cls % len(PAIRS)
    pair_i = sum(jnp.where(pair == n, i, 0) for n, (i, _) in enumerate(PAIRS))
    pair_j = sum(jnp.where(pair == n, j, 0) for n, (_, j) in enumerate(PAIRS))
    ea = grp * EXPERTS_PER_GROUP + pair_i
    eb = grp * EXPERTS_PER_GROUP + pair_j
    tok_s = tok_s.reshape(n_tiles, tile)
    valid = tok_s >= 0
    src = jnp.where(valid, tok_s, 0)
    t_ids = jnp.arange(n_tiles, dtype=jnp.int32)[:, None]
    r_ids = jnp.arange(tile, dtype=jnp.int32)[None, :]
    dst = jnp.where(valid, tok_s, n_tokens + (t_ids % 2) * tile + r_ids)
    dst_prev = jnp.concatenate([n_tokens + tile + r_ids, dst[:-1]], axis=0)
    w = jnp.stack([wlo_s.reshape(n_tiles, tile), whi_s.reshape(n_tiles, tile)], axis=1)
    shape = (n_tiles, 1, tile)
    return ea, eb, src.reshape(shape), dst.reshape(shape), dst_prev.reshape(shape), w


N_DMA_CHUNKS = 8


def _experts_kernel(ea_ref, eb_ref, src_ref, src1_ref, src2_ref, dst_ref, dstp_ref, w_ref, h_hbm,
                    wga_ref, wua_ref, wda_ref, wgb_ref, wub_ref, wdb_ref, y_in_hbm, y_hbm,
                    xbuf, ybuf, xb_ref, ha_ref, gsem, ssem):
    del eb_ref, y_in_hbm
    t = pl.program_id(0)
    n_t = pl.num_programs(0)
    tile = xbuf.shape[1]
    slot = t % 2
    other = 1 - slot
    gslot = lax.rem(t, 3)
    gnext = lax.rem(t + 2, 3)
    chunk = tile // N_DMA_CHUNKS

    def gather_rows(idx_ref, s, rows):
        for r in rows:
            pltpu.make_async_copy(h_hbm.at[pl.ds(idx_ref[0, 0, r], 1), :],
                                  xbuf.at[s, pl.ds(r, 1), :], gsem.at[s]).start()

    def wait_gather(s):
        pltpu.make_async_copy(h_hbm.at[pl.ds(0, tile), :], xbuf.at[s], gsem.at[s]).wait()

    def scatter_rows(idx_ref, s, rows):
        for r in rows:
            pltpu.make_async_copy(ybuf.at[s, pl.ds(r, 1), :],
                                  y_hbm.at[pl.ds(idx_ref[0, 0, r], 1), :], ssem.at[s]
                                  ).start(priority=1)

    def wait_scatter(s):
        pltpu.make_async_copy(ybuf.at[s], y_hbm.at[pl.ds(0, tile), :], ssem.at[s]).wait()

    @pl.when(t == 0)
    def _():
        ybuf[...] = jnp.zeros(ybuf.shape, F32)
        gather_rows(src_ref, 0, range(tile))
        gather_rows(src1_ref, 1, range(tile))

    wait_gather(gslot)
    xb_ref[...] = xbuf[gslot].astype(BF16)
    x = xb_ref[...]
    wrow = w_ref[0]

    def issue(c):
        rows = range(c * chunk, (c + 1) * chunk)
        scatter_rows(dstp_ref, other, rows)
        gather_rows(src2_ref, gnext, rows)

    def column(row):
        c = jnp.broadcast_to(row, (V7X_LANES, tile)).T
        return jnp.concatenate([c] * (D_EXPERT // V7X_LANES), axis=1)

    def hidden(wg_ref, wu_ref, gate, c0):
        issue(c0)
        a = jnp.dot(x, wg_ref[0, 0], preferred_element_type=F32)
        issue(c0 + 1)
        u = jnp.dot(x, wu_ref[0, 0], preferred_element_type=F32)
        return ((a * jax.nn.sigmoid(a)) * u * gate).astype(BF16)

    always = ea_ref[t] >= 0

    @pl.when(always)
    def _():
        issue(0)
        issue(1)
        ha_ref[...] = hidden(wga_ref, wua_ref, column(wrow[0:1]), 2)

    @pl.when(t >= 1)
    def _():
        wait_scatter(slot)

    @pl.when(always)
    def _():
        hb = hidden(wgb_ref, wub_ref, column(wrow[1:2]), 4)
        issue(6)
        y = jnp.dot(ha_ref[...], wda_ref[0, 0], preferred_element_type=F32)
        issue(7)
        y = y + jnp.dot(hb, wdb_ref[0, 0], preferred_element_type=F32)
        ybuf[slot] = y

    @pl.when(t == n_t - 1)
    def _():
        scatter_rows(dst_ref, slot, range(tile))
        wait_scatter(other)
        wait_scatter(slot)
        wait_gather(lax.rem(t + 1, 3))
        wait_gather(gnext)


def _experts(h2_all, y_buf, disp, layer, wg, wu, wd):
    ea, eb, src, dst, dst_prev, w = disp
    n_tokens = h2_all.shape[0]
    n_tiles = src.shape[0]
    tile = MOE_TILE
    idx_spec = lambda fn: pl.BlockSpec((1, 1, tile), fn, memory_space=pltpu.SMEM)
    wspec = lambda shape, sel: pl.BlockSpec((1, 1) + shape, sel)
    grid_spec = pltpu.PrefetchScalarGridSpec(
        num_scalar_prefetch=2,
        grid=(n_tiles,),
        in_specs=[
            idx_spec(lambda t, ea, eb: (t, 0, 0)),
            idx_spec(lambda t, ea, eb: (jnp.minimum(t + 1, n_tiles - 1), 0, 0)),
            idx_spec(lambda t, ea, eb: (jnp.minimum(t + 2, n_tiles - 1), 0, 0)),
            idx_spec(lambda t, ea, eb: (t, 0, 0)),
            idx_spec(lambda t, ea, eb: (t, 0, 0)),
            pl.BlockSpec((1, 2, tile), lambda t, ea, eb: (t, 0, 0)),
            pl.BlockSpec(memory_space=pl.ANY),
            wspec((D_MODEL, D_EXPERT), lambda t, ea, eb: (layer, ea[t], 0, 0)),
            wspec((D_MODEL, D_EXPERT), lambda t, ea, eb: (layer, ea[t], 0, 0)),
            wspec((D_EXPERT, D_MODEL), lambda t, ea, eb: (layer, ea[t], 0, 0)),
            wspec((D_MODEL, D_EXPERT), lambda t, ea, eb: (layer, eb[t], 0, 0)),
            wspec((D_MODEL, D_EXPERT), lambda t, ea, eb: (layer, eb[t], 0, 0)),
            wspec((D_EXPERT, D_MODEL), lambda t, ea, eb: (layer, eb[t], 0, 0)),
            pl.BlockSpec(memory_space=pl.ANY),
        ],
        out_specs=pl.BlockSpec(memory_space=pl.ANY),
        scratch_shapes=[
            pltpu.VMEM((3, tile, D_MODEL), F32),
            pltpu.VMEM((2, tile, D_MODEL), F32),
            pltpu.VMEM((tile, D_MODEL), BF16),
            pltpu.VMEM((tile, D_EXPERT), BF16),
            pltpu.SemaphoreType.DMA((3,)),
            pltpu.SemaphoreType.DMA((2,)),
        ],
    )
    return pl.pallas_call(
        _experts_kernel,
        grid_spec=grid_spec,
        out_shape=jax.ShapeDtypeStruct((n_tokens + 2 * tile, D_MODEL), F32),
        input_output_aliases={15: 0},
        compiler_params=_params(("arbitrary",)),
        name="experts",
    )(ea, eb, src, src, src, dst, dst_prev, w, h2_all, wg, wu, wd, wg, wu, wd, y_buf)


def _rope_tables(t):
    half = HEAD_DIM // 2
    inv_freq = ROPE_THETA ** (-jnp.arange(0, half, 2, dtype=F32) / half)
    pos = jnp.arange(t, dtype=jnp.int32)
    ang_r = (pos // GRID_W).astype(F32)[None, :] * inv_freq[:, None]
    ang_c = (pos % GRID_W).astype(F32)[None, :] * inv_freq[:, None]
    return jnp.cos(ang_r), jnp.sin(ang_r), jnp.cos(ang_c), jnp.sin(ang_c)


def kernel(x_prompt, x_sample, cache_k, cache_v, c, c_ctx, norm1_g, norm2_g, ada_w, ada_b, w_in,
           q_norm_g, k_norm_g, pool_w, pool_scale, w_out, router_w, router_bias, expert_w_gate,
           expert_w_up, expert_w_down):
    n_ctx, t_ctx, _ = x_prompt.shape
    n_lat, t_lat, _ = x_sample.shape
    past = cache_k.shape[2]
    tok_ctx = n_ctx * t_ctx
    n_tokens = tok_ctx + n_lat * t_lat
    n_tiles = n_tokens // MOE_TILE + N_CLASSES
    tm_lat = 512

    cond = jnp.concatenate([c_ctx[None, :], c,
                            jnp.zeros((MOD_ROWS - 1 - n_lat, D_MODEL), F32)], axis=0)
    mod = _modulation(cond, ada_w, ada_b).reshape(DEPTH, MOD_ROWS, 1, 6 * D_MODEL)
    rope_tabs = _rope_tables(t_lat)

    wrT = router_w.T
    wr_hi = wrT.astype(BF16)
    wr2 = jnp.concatenate([wr_hi, (wrT - wr_hi.astype(F32)).astype(BF16)], axis=0)
    rb = router_bias.reshape(N_EXPERTS, 1)
    wg, wu, wd = _to_bf16(expert_w_gate), _to_bf16(expert_w_up), _to_bf16(expert_w_down)

    xp, xs = x_prompt, x_sample
    h2_all = jnp.zeros((n_tokens, D_MODEL), F32)
    y_all = jnp.zeros((n_tokens + 2 * MOE_TILE, D_MODEL), F32)
    new_k, new_v = [], []
    for l in range(DEPTH):
        w_in_b = w_in[l].astype(BF16)
        wqkvT = w_in_b[:, :QKV_WIDTH].T
        wp = w_in_b[:, QKV_WIDTH:]
        wo = w_out[l].astype(BF16)
        wo_top, wo_bot = wo[:ATTN_WIDTH], wo[ATTN_WIDTH:]
        pw = pool_w[l].astype(BF16)
        ps = pool_scale[l].reshape(1, POOL_WIDTH)
        g1 = norm1_g[l].reshape(1, D_MODEL)
        g2 = norm2_g[l].reshape(1, D_MODEL)
        gq = q_norm_g[l].reshape(HEAD_DIM, 1)
        gk = k_norm_g[l].reshape(HEAD_DIM, 1)
        mod_l = mod[l]
        res_ctx = None if l == 0 else (y_all, 0, mod[l - 1])
        res_lat = None if l == 0 else (y_all, tok_ctx, mod[l - 1])

        outs = _pre(xp, res_ctx, mod_l, 0, False, g1, wqkvT, wp, gq, gk, None, t_ctx, True)
        if l == 0:
            qT, k, vT, p, k32, v32 = outs
        else:
            qT, k, vT, p, xp, k32, v32 = outs
        aT = _attention(qT, k, vT, t_ctx, t_ctx, 1)
        xp, h2_all, route_ctx = _post(xp, aT, p, h2_all, 0, mod_l, 0, False, wo_top, wo_bot,
                                      pw, ps, g2, wr2, rb, t_ctx)
        new_k.append(k32.reshape(n_ctx, t_ctx, N_KV_HEADS, HEAD_DIM))
        new_v.append(v32.reshape(n_ctx, t_ctx, N_KV_HEADS, HEAD_DIM))

        outs = _pre(xs, res_lat, mod_l, 1, True, g1, wqkvT, wp, gq, gk, rope_tabs, tm_lat, False)
        if l == 0:
            qT, k, vT, p = outs
        else:
            qT, k, vT, p, xs = outs
        ck = cache_k[:, l].reshape(n_lat, past, KV_WIDTH).astype(QK8)
        cvT = cache_v[:, l].reshape(n_lat, past, KV_WIDTH).astype(QK8).transpose(0, 2, 1)
        k_all = jnp.concatenate([ck, k], axis=1)
        vT_all = jnp.concatenate([cvT, vT], axis=2)
        aT = _attention(qT, k_all, vT_all, 256, 512, 3)
        xs, h2_all, route_lat = _post(xs, aT, p, h2_all, tok_ctx, mod_l, 1, True, wo_top,
                                      wo_bot, pw, ps, g2, wr2, rb, 256)

        route_rows = jnp.concatenate(
            [route_ctx.transpose(0, 2, 1).reshape(tok_ctx, ROUTE_ROWS),
             route_lat.transpose(0, 2, 1).reshape(n_tokens - tok_ctx, ROUTE_ROWS)], axis=0)
        y_all = _experts(h2_all, y_all, _dispatch(route_rows, n_tokens, n_tiles), l, wg, wu, wd)

    mod_l = mod[DEPTH - 1]
    xp = _finish(xp, y_all, 0, mod_l, 0, False, t_ctx)
    xs = _finish(xs, y_all, tok_ctx, mod_l, 1, True, tm_lat)
    return xp, xs, jnp.stack(new_k, axis=1), jnp.stack(new_v, axis=1)
```

```python
import functools

import jax
import jax.numpy as jnp
from jax import lax
from jax.experimental import pallas as pl
from jax.experimental.pallas import tpu as pltpu

D_MODEL = 1024
DEPTH = 4
GRID_W = 64
HEAD_DIM = 64
N_Q_HEADS = 8
N_KV_HEADS = 2
Q_PER_KV = N_Q_HEADS // N_KV_HEADS
ATTN_WIDTH = N_Q_HEADS * HEAD_DIM
KV_WIDTH = N_KV_HEADS * HEAD_DIM
QKV_WIDTH = ATTN_WIDTH + 2 * KV_WIDTH
POOL_WINDOWS = (2, 4, 8, 16)
POOL_WIDTH = D_MODEL - ATTN_WIDTH
POOL_GROUP_WIDTH = POOL_WIDTH // len(POOL_WINDOWS)
N_EXPERTS = 16
N_EXPERT_GROUPS = 4
EXPERTS_PER_GROUP = N_EXPERTS // N_EXPERT_GROUPS
D_EXPERT = 512
ROPE_THETA = 10000.0
EPS = 1e-6

Q_SCALE = HEAD_DIM ** -0.5 * 1.4426950408889634
DENOM_ROWS = 32
P_SHIFT = 8.0
MOD_ROWS = 16
POOL_HALO = 16
V7X_LANES = 128
VMEM_LIMIT = 48 * 1024 * 1024

PAIRS = tuple((i, j) for i in range(EXPERTS_PER_GROUP) for j in range(i + 1, EXPERTS_PER_GROUP))
N_CLASSES = N_EXPERT_GROUPS * len(PAIRS)
ROUTE_ROWS = 8
MOE_TILE = 256

F32 = jnp.float32
BF16 = jnp.bfloat16
QK8 = jnp.float8_e4m3fn
NT_DIMS = (((1,), (1,)), ((), ()))
TN_DIMS = (((0,), (0,)), ((), ()))


def _params(sem):
    return pltpu.CompilerParams(dimension_semantics=sem, vmem_limit_bytes=VMEM_LIMIT)


def _mod_kernel(c_ref, w_ref, b_ref, o_ref):
    c = c_ref[...]
    a = c * jax.nn.sigmoid(c)
    o_ref[0] = jnp.dot(a, w_ref[0], preferred_element_type=F32,
                       precision=lax.Precision.HIGHEST) + b_ref[0]


def _modulation(cond, ada_w, ada_b):
    tn = 1536
    n_out = ada_w.shape[2]
    return pl.pallas_call(
        _mod_kernel,
        grid=(DEPTH, n_out // tn),
        in_specs=[
            pl.BlockSpec((MOD_ROWS, D_MODEL), lambda l, n: (0, 0)),
            pl.BlockSpec((1, D_MODEL, tn), lambda l, n: (l, 0, n)),
            pl.BlockSpec((1, 1, tn), lambda l, n: (l, 0, n)),
        ],
        out_specs=pl.BlockSpec((1, MOD_ROWS, tn), lambda l, n: (l, 0, n)),
        out_shape=jax.ShapeDtypeStruct((DEPTH, MOD_ROWS, n_out), F32),
        compiler_params=_params(("arbitrary", "arbitrary")),
        name="modulation",
    )(cond, ada_w, ada_b.reshape(DEPTH, 1, n_out))


def _cast_kernel(x_ref, o_ref):
    o_ref[...] = x_ref[...].astype(o_ref.dtype)


def _to_bf16(w):
    blk = pl.BlockSpec((1, 1) + w.shape[2:], lambda l, e: (l, e, 0, 0))
    return pl.pallas_call(
        _cast_kernel, grid=w.shape[:2], in_specs=[blk], out_specs=blk,
        out_shape=jax.ShapeDtypeStruct(w.shape, BF16),
        compiler_params=_params(("parallel", "parallel")), name="cast_bf16")(w)


def _rms_modulate(x, g, shift, scale):
    ms = jnp.mean(x * x, axis=-1, keepdims=True)
    h = x * lax.rsqrt(ms + EPS) * g
    return h * (1.0 + scale) + shift


def _pre_kernel(*refs, rope, emit_f32, residual):
    refs = list(refs)
    x_ref = refs.pop(0)
    if residual:
        y_ref, modp_ref = refs.pop(0), refs.pop(0)
    mod_ref, g1_ref, wqkv_ref, wp_ref, gq_ref, gk_ref = refs[:6]
    refs = refs[6:]
    if rope:
        cr_ref, sr_ref, cc_ref, sc_ref = refs[:4]
        refs = refs[4:]
    qT_ref, k_ref, vT_ref, p_ref = refs[:4]
    refs = refs[4:]
    tm = x_ref.shape[1]
    x = x_ref[0]
    if residual:
        x = x + modp_ref[0][:, 5 * D_MODEL:6 * D_MODEL] * y_ref[...]
        refs.pop(0)[0] = x
    m = mod_ref[0]
    h = _rms_modulate(x, g1_ref[...], m[:, 0:D_MODEL], m[:, D_MODEL:2 * D_MODEL])
    hb = h.astype(BF16)
    zT = lax.dot_general(wqkv_ref[...], hb, NT_DIMS, preferred_element_type=F32)
    p_ref[0] = jnp.dot(hb, wp_ref[...], preferred_element_type=F32).astype(BF16)

    def norm_rope(zt, n_heads, g_ref):
        z3 = zt.reshape(n_heads, HEAD_DIM, tm)
        ms = jnp.mean(z3 * z3, axis=1, keepdims=True)
        y = z3 * lax.rsqrt(ms + EPS) * g_ref[...][None]
        if rope:
            cr, sr = cr_ref[...][None], sr_ref[...][None]
            cc, sc = cc_ref[...][None], sc_ref[...][None]
            x1r, x2r, x1c, x2c = y[:, 0:16], y[:, 16:32], y[:, 32:48], y[:, 48:64]
            y = jnp.concatenate([x1r * cr - x2r * sr, x1r * sr + x2r * cr,
                                 x1c * cc - x2c * sc, x1c * sc + x2c * cc], axis=1)
        return y.reshape(n_heads * HEAD_DIM, tm)

    qT = norm_rope(zT[0:ATTN_WIDTH], N_Q_HEADS, gq_ref)
    qT_ref[0] = (qT * Q_SCALE).astype(BF16)
    kT = norm_rope(zT[ATTN_WIDTH:ATTN_WIDTH + KV_WIDTH], N_KV_HEADS, gk_ref)
    k_tok = kT.T
    k_ref[0] = k_tok.astype(QK8)
    vT = zT[ATTN_WIDTH + KV_WIDTH:QKV_WIDTH]
    vT_ref[0] = vT.astype(QK8)
    if emit_f32:
        k32_ref, v32_ref = refs[:2]
        k32_ref[0] = k_tok
        v32_ref[0] = vT.T


def _pre(x, res, mod_l, row0, per_batch_mod, g1, wqkvT, wp, gq, gk, rope_tabs, tm, emit_f32):
    b, t, _ = x.shape
    rope = rope_tabs is not None
    mod_map = (lambda bi, i: (row0 + bi, 0, 0)) if per_batch_mod else (lambda bi, i: (row0, 0, 0))
    tok_spec = pl.BlockSpec((1, tm, D_MODEL), lambda bi, i: (bi, i, 0))
    in_specs, args = [tok_spec], [x]
    if res is not None:
        y_all, row_off, mod_prev = res
        blk0, per_b = row_off // tm, t // tm
        in_specs += [pl.BlockSpec((tm, D_MODEL), lambda bi, i: (blk0 + bi * per_b + i, 0)),
                     pl.BlockSpec((1, 1, 6 * D_MODEL), mod_map)]
        args += [y_all, mod_prev]
    in_specs += [
        pl.BlockSpec((1, 1, 6 * D_MODEL), mod_map),
        pl.BlockSpec((1, D_MODEL), lambda bi, i: (0, 0)),
        pl.BlockSpec((QKV_WIDTH, D_MODEL), lambda bi, i: (0, 0)),
        pl.BlockSpec((D_MODEL, POOL_WIDTH), lambda bi, i: (0, 0)),
        pl.BlockSpec((HEAD_DIM, 1), lambda bi, i: (0, 0)),
        pl.BlockSpec((HEAD_DIM, 1), lambda bi, i: (0, 0)),
    ]
    args += [mod_l, g1, wqkvT, wp, gq, gk]
    if rope:
        in_specs += [pl.BlockSpec((16, tm), lambda bi, i: (0, i))] * 4
        args += list(rope_tabs)
    out_shape = [
        jax.ShapeDtypeStruct((b, ATTN_WIDTH, t), BF16),
        jax.ShapeDtypeStruct((b, t, KV_WIDTH), QK8),
        jax.ShapeDtypeStruct((b, KV_WIDTH, t), QK8),
        jax.ShapeDtypeStruct((b, t, POOL_WIDTH), BF16),
    ]
    out_specs = [
        pl.BlockSpec((1, ATTN_WIDTH, tm), lambda bi, i: (bi, 0, i)),
        pl.BlockSpec((1, tm, KV_WIDTH), lambda bi, i: (bi, i, 0)),
        pl.BlockSpec((1, KV_WIDTH, tm), lambda bi, i: (bi, 0, i)),
        pl.BlockSpec((1, tm, POOL_WIDTH), lambda bi, i: (bi, i, 0)),
    ]
    if res is not None:
        out_shape.append(jax.ShapeDtypeStruct((b, t, D_MODEL), F32))
        out_specs.append(tok_spec)
    if emit_f32:
        out_shape += [jax.ShapeDtypeStruct((b, t, KV_WIDTH), F32)] * 2
        out_specs += [pl.BlockSpec((1, tm, KV_WIDTH), lambda bi, i: (bi, i, 0))] * 2
    return pl.pallas_call(
        functools.partial(_pre_kernel, rope=rope, emit_f32=emit_f32, residual=res is not None),
        grid=(b, t // tm),
        in_specs=in_specs,
        out_specs=out_specs,
        out_shape=out_shape,
        compiler_params=_params(("parallel", "parallel")),
        name="pre_attention",
    )(*args)


def _finish_kernel(x_ref, y_ref, mod_ref, o_ref):
    o_ref[0] = x_ref[0] + mod_ref[0][:, 5 * D_MODEL:6 * D_MODEL] * y_ref[...]


def _finish(xmid, y_all, row_off, mod_l, row0, per_batch_mod, tm):
    b, t, _ = xmid.shape
    blk0, per_b = row_off // tm, t // tm
    mod_map = (lambda bi, i: (row0 + bi, 0, 0)) if per_batch_mod else (lambda bi, i: (row0, 0, 0))
    tok_spec = pl.BlockSpec((1, tm, D_MODEL), lambda bi, i: (bi, i, 0))
    return pl.pallas_call(
        _finish_kernel,
        grid=(b, t // tm),
        in_specs=[tok_spec,
                  pl.BlockSpec((tm, D_MODEL), lambda bi, i: (blk0 + bi * per_b + i, 0)),
                  pl.BlockSpec((1, 1, 6 * D_MODEL), mod_map)],
        out_specs=tok_spec,
        out_shape=jax.ShapeDtypeStruct((b, t, D_MODEL), F32),
        compiler_params=_params(("parallel", "parallel")),
        name="moe_residual",
    )(xmid, y_all, mod_l)


def _attn_kernel(qT_ref, k_ref, vT_ref, o_ref, qpad_ref, acc_ref, m_ref, *, tk, unroll):
    tq = qT_ref.shape[2]
    s_len = k_ref.shape[1]
    n_pairs = N_Q_HEADS // 2
    zeros = jnp.zeros((HEAD_DIM, tq), QK8)
    for h in range(N_Q_HEADS):
        qh = qT_ref[0, h * HEAD_DIM:(h + 1) * HEAD_DIM, :].astype(QK8)
        qpad_ref[h // 2, :, (h % 2) * tq:(h % 2 + 1) * tq] = jnp.concatenate(
            [qh, zeros] if h < Q_PER_KV else [zeros, qh], axis=0)
    m_ref[...] = jnp.full(m_ref.shape, -jnp.inf, F32)
    acc_ref[...] = jnp.zeros(acc_ref.shape, F32)

    n_items = unroll * n_pairs

    def step(t, carry):
        ones = jnp.ones((DENOM_ROWS, tk), QK8)

        def keys(u):
            off = pl.multiple_of((t * unroll + u) * tk, tk)
            return k_ref[0, pl.ds(off, tk), :]

        def values(u, j):
            off = pl.multiple_of((t * unroll + u) * tk, tk)
            vb = vT_ref[0, j * HEAD_DIM:(j + 1) * HEAD_DIM, pl.ds(off, tk)]
            return jnp.concatenate([vb, ones], axis=0)

        def scores(i):
            return jnp.dot(keys(i // n_pairs), qpad_ref[i % n_pairs],
                           preferred_element_type=F32).astype(BF16)

        queue = [scores(0), scores(1)]
        for i in range(n_items):
            u, pr = divmod(i, n_pairs)
            j = (2 * pr) // Q_PER_KV
            s = queue.pop(0)
            m_old = m_ref[pr]
            m_new = jnp.maximum(m_old, jnp.max(s, axis=0, keepdims=True).astype(F32))
            c_old = (m_old - P_SHIFT).astype(BF16)
            c_new = (m_new - P_SHIFT).astype(BF16)
            alpha = jnp.exp2(c_old.astype(F32) - c_new.astype(F32))
            p = jnp.exp2(jnp.minimum(s - c_new, P_SHIFT)).astype(QK8)
            if i + 2 < n_items:
                queue.append(scores(i + 2))
            pv = jnp.dot(values(u, j), p, preferred_element_type=F32)
            acc_ref[pr] = alpha * acc_ref[pr] + pv
            m_ref[pr] = m_new
        return carry

    lax.fori_loop(0, s_len // (tk * unroll), step, 0)
    for h in range(N_Q_HEADS):
        lanes = slice((h % 2) * tq, (h % 2 + 1) * tq)
        o_ref[0, h * HEAD_DIM:(h + 1) * HEAD_DIM, :] = (
            acc_ref[h // 2, 0:HEAD_DIM, lanes]
            / acc_ref[h // 2, HEAD_DIM:HEAD_DIM + 1, lanes]).astype(BF16)


def _attention(qT, k_all, vT_all, tq, tk, unroll):
    b, _, t = qT.shape
    s_len = k_all.shape[1]
    return pl.pallas_call(
        functools.partial(_attn_kernel, tk=tk, unroll=unroll),
        grid=(b, t // tq),
        in_specs=[
            pl.BlockSpec((1, ATTN_WIDTH, tq), lambda bi, i: (bi, 0, i)),
            pl.BlockSpec((1, s_len, KV_WIDTH), lambda bi, i: (bi, 0, 0)),
            pl.BlockSpec((1, KV_WIDTH, s_len), lambda bi, i: (bi, 0, 0)),
        ],
        out_specs=pl.BlockSpec((1, ATTN_WIDTH, tq), lambda bi, i: (bi, 0, i)),
        out_shape=jax.ShapeDtypeStruct((b, ATTN_WIDTH, t), BF16),
        scratch_shapes=[
            pltpu.VMEM((N_Q_HEADS // 2, 2 * HEAD_DIM, 2 * tq), QK8),
            pltpu.VMEM((N_Q_HEADS // 2, HEAD_DIM + DENOM_ROWS, 2 * tq), F32),
            pltpu.VMEM((N_Q_HEADS // 2, 1, 2 * tq), F32),
        ],
        compiler_params=_params(("parallel", "parallel")),
        name="attention",
    )(qT, k_all, vT_all)


def _route(logits, bias):
    aff = jax.nn.sigmoid(logits)
    sel = aff + bias
    eidx = lax.broadcasted_iota(jnp.int32, sel.shape, 0).astype(F32)
    scores = []
    for g in range(N_EXPERT_GROUPS):
        r = [sel[EXPERTS_PER_GROUP * g + i:EXPERTS_PER_GROUP * g + i + 1, :]
             for i in range(EXPERTS_PER_GROUP)]
        best = None
        for i, j in PAIRS:
            sij = r[i] + r[j]
            best = sij if best is None else jnp.maximum(best, sij)
        scores.append(best)
    gmax = functools.reduce(jnp.maximum, scores)
    gstar = jnp.where(scores[0] == gmax, 0.0,
                      jnp.where(scores[1] == gmax, 1.0, jnp.where(scores[2] == gmax, 2.0, 3.0)))
    in_grp = jnp.floor(eidx * (1.0 / EXPERTS_PER_GROUP)) == gstar
    cand = jnp.where(in_grp, sel, -jnp.inf)
    m1 = jnp.max(cand, axis=0, keepdims=True)
    idx1 = jnp.min(jnp.where(cand == m1, eidx, float(N_EXPERTS)), axis=0, keepdims=True)
    cand2 = jnp.where(eidx == idx1, -jnp.inf, cand)
    m2 = jnp.max(cand2, axis=0, keepdims=True)
    idx2 = jnp.min(jnp.where(cand2 == m2, eidx, float(N_EXPERTS)), axis=0, keepdims=True)
    lo, hi = jnp.minimum(idx1, idx2), jnp.maximum(idx1, idx2)
    a_lo = jnp.sum(jnp.where(eidx == lo, aff, 0.0), axis=0, keepdims=True)
    a_hi = jnp.sum(jnp.where(eidx == hi, aff, 0.0), axis=0, keepdims=True)
    den = a_lo + a_hi
    i_in = lo - EXPERTS_PER_GROUP * gstar
    j_in = hi - EXPERTS_PER_GROUP * gstar
    first = jnp.where(i_in == 0.0, 0.0, jnp.where(i_in == 1.0, 2.0, 3.0))
    cls = gstar * len(PAIRS) + first + j_in - 1.0
    pad = jnp.zeros((ROUTE_ROWS - 3, logits.shape[1]), F32)
    return jnp.concatenate([cls, a_lo / den, a_hi / den, pad], axis=0)


def _post_kernel(x_ref, aT_ref, p_ref, pprev_ref, pnext_ref, mod_ref, wot_ref, wob_ref, pw_ref,
                 ps_ref, g2_ref, wr_ref, rb_ref, h2_in_ref, xmid_ref, h2_ref, route_ref, *, seq_len):
    del h2_in_ref
    tm = x_ref.shape[1]
    i = pl.program_id(1)
    n_tiles = seq_len // tm
    m = mod_ref[0]
    gate_msa = m[:, 2 * D_MODEL:3 * D_MODEL]
    shift2 = m[:, 3 * D_MODEL:4 * D_MODEL]
    scale2 = m[:, 4 * D_MODEL:5 * D_MODEL]

    pc = p_ref[0].astype(F32)
    prev = jnp.where(i > 0, pprev_ref[0].astype(F32), 0.0)
    nxt = jnp.where(i < n_tiles - 1, pnext_ref[0].astype(F32), 0.0)
    ext = jnp.concatenate([prev, pc, nxt], axis=0)
    t_seq = lax.broadcasted_iota(jnp.int32, (tm, POOL_GROUP_WIDTH), 0) + i * tm
    pools = []
    for g, w in enumerate(POOL_WINDOWS):
        lanes = slice(g * POOL_GROUP_WIDTH, (g + 1) * POOL_GROUP_WIDTH)
        a = ext[:, lanes]
        span = 1
        while span < w:
            a = a[:-span] + a[span:]
            span *= 2
        start = POOL_HALO - w // 2
        wsum = a[start:start + tm]
        lo = jnp.maximum(t_seq - w // 2, 0)
        hi = jnp.minimum(t_seq + w // 2, seq_len)
        pooled = wsum / (hi - lo).astype(F32) - pc[:, lanes]
        mixed = jnp.dot(pooled.astype(BF16), pw_ref[g], preferred_element_type=F32)
        pools.append(mixed * ps_ref[:, lanes])
    pool = jnp.concatenate(pools, axis=1).astype(BF16)

    proj = lax.dot_general(aT_ref[0], wot_ref[...], TN_DIMS, preferred_element_type=F32)
    proj = proj + jnp.dot(pool, wob_ref[...], preferred_element_type=F32)
    xm = x_ref[0] + gate_msa * proj
    xmid_ref[0] = xm

    h2 = _rms_modulate(xm, g2_ref[...], shift2, scale2)
    h2_ref[...] = h2
    h_hi = h2.astype(BF16)
    h_lo = (h2 - h_hi.astype(F32)).astype(BF16)
    r = lax.dot_general(wr_ref[...], jnp.concatenate([h_hi, h_lo], axis=0), NT_DIMS,
                        preferred_element_type=F32)
    logits = r[0:N_EXPERTS, 0:tm] + r[0:N_EXPERTS, tm:2 * tm] + r[N_EXPERTS:2 * N_EXPERTS, 0:tm]
    route_ref[0] = _route(logits, rb_ref[...])


def _post(x, aT, p, h2_buf, row_off, mod_l, row0, per_batch_mod, wo_top, wo_bot, pool_w,
          pool_scale, g2, wr2, rb, tm):
    b, t, _ = x.shape
    n_total = h2_buf.shape[0]
    hb = tm // POOL_HALO
    last = t // POOL_HALO - 1
    blk0, per_b = row_off // tm, t // tm
    mod_map = (lambda bi, i: (row0 + bi, 0, 0)) if per_batch_mod else (lambda bi, i: (row0, 0, 0))
    const2 = lambda bi, i: (0, 0)
    tok_spec = pl.BlockSpec((1, tm, D_MODEL), lambda bi, i: (bi, i, 0))
    in_specs = [
        tok_spec,
        pl.BlockSpec((1, ATTN_WIDTH, tm), lambda bi, i: (bi, 0, i)),
        pl.BlockSpec((1, tm, POOL_WIDTH), lambda bi, i: (bi, i, 0)),
        pl.BlockSpec((1, POOL_HALO, POOL_WIDTH), lambda bi, i: (bi, jnp.maximum(i * hb - 1, 0), 0)),
        pl.BlockSpec((1, POOL_HALO, POOL_WIDTH), lambda bi, i: (bi, jnp.minimum((i + 1) * hb, last), 0)),
        pl.BlockSpec((1, 1, 6 * D_MODEL), mod_map),
        pl.BlockSpec((ATTN_WIDTH, D_MODEL), const2),
        pl.BlockSpec((POOL_WIDTH, D_MODEL), const2),
        pl.BlockSpec((len(POOL_WINDOWS), POOL_GROUP_WIDTH, POOL_GROUP_WIDTH), lambda bi, i: (0, 0, 0)),
        pl.BlockSpec((1, POOL_WIDTH), const2),
        pl.BlockSpec((1, D_MODEL), const2),
        pl.BlockSpec((2 * N_EXPERTS, D_MODEL), const2),
        pl.BlockSpec((N_EXPERTS, 1), const2),
        pl.BlockSpec(memory_space=pl.ANY),
    ]
    args = [x, aT, p, p, p, mod_l, wo_top, wo_bot, pool_w, pool_scale, g2, wr2, rb, h2_buf]
    aliases = {len(args) - 1: 1}
    return pl.pallas_call(
        functools.partial(_post_kernel, seq_len=t),
        grid=(b, t // tm),
        in_specs=in_specs,
        out_specs=[
            tok_spec,
            pl.BlockSpec((tm, D_MODEL), lambda bi, i: (blk0 + bi * per_b + i, 0)),
            pl.BlockSpec((1, ROUTE_ROWS, tm), lambda bi, i: (bi, 0, i)),
        ],
        out_shape=[
            jax.ShapeDtypeStruct((b, t, D_MODEL), F32),
            jax.ShapeDtypeStruct((n_total, D_MODEL), F32),
            jax.ShapeDtypeStruct((b, ROUTE_ROWS, t), F32),
        ],
        input_output_aliases=aliases,
        compiler_params=_params(("arbitrary", "arbitrary")),
        name="post_attention",
    )(*args)


def _dispatch(route_rows, n_tokens, n_tiles):
    tile = MOE_TILE
    n_pad = n_tiles * tile - n_tokens
    classes = jnp.arange(N_CLASSES, dtype=jnp.int32)
    cls = route_rows[:, 0].astype(jnp.int32)
    counts = jnp.sum((cls[:, None] == classes[None, :]).astype(jnp.int32), axis=0)
    need_end = jnp.cumsum((-counts) % tile)
    pad_ids = jnp.arange(n_pad, dtype=jnp.int32)
    pad_cls = jnp.sum((need_end[None, :] <= pad_ids[:, None]).astype(jnp.int32), axis=1)
    cls_s, tok_s, wlo_s, whi_s = lax.sort(
        (jnp.concatenate([cls, pad_cls]),
         jnp.concatenate([jnp.arange(n_tokens, dtype=jnp.int32), jnp.full((n_pad,), -1, jnp.int32)]),
         jnp.concatenate([route_rows[:, 1], jnp.zeros((n_pad,), F32)]),
         jnp.concatenate([route_rows[:, 2], jnp.zeros((n_pad,), F32)])),
        num_keys=1, is_stable=True)
    tile_cls = jnp.minimum(cls_s.reshape(n_tiles, tile)[:, 0], N_CLASSES - 1)
    grp, pair = tile_cls // len(PAIRS), tile_cls % len(PAIRS)
    pair_i = sum(jnp.where(pair == n, i, 0) for n, (i, _) in enumerate(PAIRS))
    pair_j = sum(jnp.where(pair == n, j, 0) for n, (_, j) in enumerate(PAIRS))
    ea = grp * EXPERTS_PER_GROUP + pair_i
    eb = grp * EXPERTS_PER_GROUP + pair_j
    tok_s = tok_s.reshape(n_tiles, tile)
    valid = tok_s >= 0
    src = jnp.where(valid, tok_s, 0)
    t_ids = jnp.arange(n_tiles, dtype=jnp.int32)[:, None]
    r_ids = jnp.arange(tile, dtype=jnp.int32)[None, :]
    dst = jnp.where(valid, tok_s, n_tokens + (t_ids % 3) * tile + r_ids)
    dst_prev = jnp.concatenate([n_tokens + 2 * tile + r_ids, dst[:-1]], axis=0)
    w = jnp.stack([wlo_s.reshape(n_tiles, tile), whi_s.reshape(n_tiles, tile)], axis=1)
    shape = (n_tiles, 1, tile)
    return ea, eb, src.reshape(shape), dst.reshape(shape), dst_prev.reshape(shape), w


N_DMA_CHUNKS = 8


def _experts_kernel(ea_ref, eb_ref, src_ref, src1_ref, src2_ref, dst_ref, dstp_ref, w_ref, h_hbm,
                    wga_ref, wua_ref, wda_ref, wgb_ref, wub_ref, wdb_ref, y_in_hbm, y_hbm,
                    xbuf, ybuf, xb_ref, ha_ref, gsem, ssem):
    del eb_ref, y_in_hbm
    t = pl.program_id(0)
    n_t = pl.num_programs(0)
    tile = xbuf.shape[1]
    gslot = lax.rem(t, 3)
    gnext = lax.rem(t + 2, 3)
    slot, other = gslot, gnext
    chunk = tile // N_DMA_CHUNKS

    def gather_rows(idx_ref, s, rows):
        for r in rows:
            pltpu.make_async_copy(h_hbm.at[pl.ds(idx_ref[0, 0, r], 1), :],
                                  xbuf.at[s, pl.ds(r, 1), :], gsem.at[s]).start()

    def wait_gather(s):
        pltpu.make_async_copy(h_hbm.at[pl.ds(0, tile), :], xbuf.at[s], gsem.at[s]).wait()

    def scatter_rows(idx_ref, s, rows):
        for r in rows:
            pltpu.make_async_copy(ybuf.at[s, pl.ds(r, 1), :],
                                  y_hbm.at[pl.ds(idx_ref[0, 0, r], 1), :], ssem.at[s]
                                  ).start(priority=1)

    def wait_scatter(s):
        pltpu.make_async_copy(ybuf.at[s], y_hbm.at[pl.ds(0, tile), :], ssem.at[s]).wait()

    @pl.when(t == 0)
    def _():
        ybuf[...] = jnp.zeros(ybuf.shape, F32)
        gather_rows(src_ref, 0, range(tile))
        gather_rows(src1_ref, 1, range(tile))

    wait_gather(gslot)
    xb_ref[...] = xbuf[gslot].astype(BF16)
    x = xb_ref[...]
    wrow = w_ref[0]

    def issue(c):
        rows = range(c * chunk, (c + 1) * chunk)
        scatter_rows(dstp_ref, other, rows)
        gather_rows(src2_ref, gnext, rows)

    def column(row):
        c = jnp.broadcast_to(row, (V7X_LANES, tile)).T
        return jnp.concatenate([c] * (D_EXPERT // V7X_LANES), axis=1)

    def hidden(wg_ref, wu_ref, gate, c0):
        issue(c0)
        a = jnp.dot(x, wg_ref[0, 0], preferred_element_type=F32)
        issue(c0 + 1)
        u = jnp.dot(x, wu_ref[0, 0], preferred_element_type=F32)
        return ((a * jax.nn.sigmoid(a)) * u * gate).astype(BF16)

    always = ea_ref[t] >= 0

    @pl.when(always)
    def _():
        issue(0)
        issue(1)
        ha_ref[...] = hidden(wga_ref, wua_ref, column(wrow[0:1]), 2)

    @pl.when(t >= 2)
    def _():
        wait_scatter(slot)

    @pl.when(always)
    def _():
        hb = hidden(wgb_ref, wub_ref, column(wrow[1:2]), 4)
        issue(6)
        y = jnp.dot(ha_ref[...], wda_ref[0, 0], preferred_element_type=F32)
        issue(7)
        y = y + jnp.dot(hb, wdb_ref[0, 0], preferred_element_type=F32)
        ybuf[slot] = y

    @pl.when(t == n_t - 1)
    def _():
        scatter_rows(dst_ref, slot, range(tile))
        wait_scatter(lax.rem(t + 1, 3))
        wait_scatter(other)
        wait_scatter(slot)
        wait_gather(lax.rem(t + 1, 3))
        wait_gather(gnext)


def _experts(h2_all, y_buf, disp, layer, wg, wu, wd):
    ea, eb, src, dst, dst_prev, w = disp
    n_tokens = h2_all.shape[0]
    n_tiles = src.shape[0]
    tile = MOE_TILE
    idx_spec = lambda fn: pl.BlockSpec((1, 1, tile), fn, memory_space=pltpu.SMEM)
    wspec = lambda shape, sel: pl.BlockSpec((1, 1) + shape, sel)
    grid_spec = pltpu.PrefetchScalarGridSpec(
        num_scalar_prefetch=2,
        grid=(n_tiles,),
        in_specs=[
            idx_spec(lambda t, ea, eb: (t, 0, 0)),
            idx_spec(lambda t, ea, eb: (jnp.minimum(t + 1, n_tiles - 1), 0, 0)),
            idx_spec(lambda t, ea, eb: (jnp.minimum(t + 2, n_tiles - 1), 0, 0)),
            idx_spec(lambda t, ea, eb: (t, 0, 0)),
            idx_spec(lambda t, ea, eb: (t, 0, 0)),
            pl.BlockSpec((1, 2, tile), lambda t, ea, eb: (t, 0, 0)),
            pl.BlockSpec(memory_space=pl.ANY),
            wspec((D_MODEL, D_EXPERT), lambda t, ea, eb: (layer, ea[t], 0, 0)),
            wspec((D_MODEL, D_EXPERT), lambda t, ea, eb: (layer, ea[t], 0, 0)),
            wspec((D_EXPERT, D_MODEL), lambda t, ea, eb: (layer, ea[t], 0, 0)),
            wspec((D_MODEL, D_EXPERT), lambda t, ea, eb: (layer, eb[t], 0, 0)),
            wspec((D_MODEL, D_EXPERT), lambda t, ea, eb: (layer, eb[t], 0, 0)),
            wspec((D_EXPERT, D_MODEL), lambda t, ea, eb: (layer, eb[t], 0, 0)),
            pl.BlockSpec(memory_space=pl.ANY),
        ],
        out_specs=pl.BlockSpec(memory_space=pl.ANY),
        scratch_shapes=[
            pltpu.VMEM((3, tile, D_MODEL), F32),
            pltpu.VMEM((3, tile, D_MODEL), F32),
            pltpu.VMEM((tile, D_MODEL), BF16),
            pltpu.VMEM((tile, D_EXPERT), BF16),
            pltpu.SemaphoreType.DMA((3,)),
            pltpu.SemaphoreType.DMA((3,)),
        ],
    )
    return pl.pallas_call(
        _experts_kernel,
        grid_spec=grid_spec,
        out_shape=jax.ShapeDtypeStruct((n_tokens + 3 * tile, D_MODEL), F32),
        input_output_aliases={15: 0},
        compiler_params=_params(("arbitrary",)),
        name="experts",
    )(ea, eb, src, src, src, dst, dst_prev, w, h2_all, wg, wu, wd, wg, wu, wd, y_buf)


def _rope_tables(t):
    half = HEAD_DIM // 2
    inv_freq = ROPE_THETA ** (-jnp.arange(0, half, 2, dtype=F32) / half)
    pos = jnp.arange(t, dtype=jnp.int32)
    ang_r = (pos // GRID_W).astype(F32)[None, :] * inv_freq[:, None]
    ang_c = (pos % GRID_W).astype(F32)[None, :] * inv_freq[:, None]
    return jnp.cos(ang_r), jnp.sin(ang_r), jnp.cos(ang_c), jnp.sin(ang_c)


def kernel(x_prompt, x_sample, cache_k, cache_v, c, c_ctx, norm1_g, norm2_g, ada_w, ada_b, w_in,
           q_norm_g, k_norm_g, pool_w, pool_scale, w_out, router_w, router_bias, expert_w_gate,
           expert_w_up, expert_w_down):
    n_ctx, t_ctx, _ = x_prompt.shape
    n_lat, t_lat, _ = x_sample.shape
    past = cache_k.shape[2]
    tok_ctx = n_ctx * t_ctx
    n_tokens = tok_ctx + n_lat * t_lat
    n_tiles = n_tokens // MOE_TILE + N_CLASSES
    tm_lat = 512

    cond = jnp.concatenate([c_ctx[None, :], c,
                            jnp.zeros((MOD_ROWS - 1 - n_lat, D_MODEL), F32)], axis=0)
    mod = _modulation(cond, ada_w, ada_b).reshape(DEPTH, MOD_ROWS, 1, 6 * D_MODEL)
    rope_tabs = _rope_tables(t_lat)

    wrT = router_w.T
    wr_hi = wrT.astype(BF16)
    wr2 = jnp.concatenate([wr_hi, (wrT - wr_hi.astype(F32)).astype(BF16)], axis=0)
    rb = router_bias.reshape(N_EXPERTS, 1)
    wg, wu, wd = _to_bf16(expert_w_gate), _to_bf16(expert_w_up), _to_bf16(expert_w_down)

    xp, xs = x_prompt, x_sample
    h2_all = jnp.zeros((n_tokens, D_MODEL), F32)
    y_all = jnp.zeros((n_tokens + 3 * MOE_TILE, D_MODEL), F32)
    new_k, new_v = [], []
    for l in range(DEPTH):
        w_in_b = w_in[l].astype(BF16)
        wqkvT = w_in_b[:, :QKV_WIDTH].T
        wp = w_in_b[:, QKV_WIDTH:]
        wo = w_out[l].astype(BF16)
        wo_top, wo_bot = wo[:ATTN_WIDTH], wo[ATTN_WIDTH:]
        pw = pool_w[l].astype(BF16)
        ps = pool_scale[l].reshape(1, POOL_WIDTH)
        g1 = norm1_g[l].reshape(1, D_MODEL)
        g2 = norm2_g[l].reshape(1, D_MODEL)
        gq = q_norm_g[l].reshape(HEAD_DIM, 1)
        gk = k_norm_g[l].reshape(HEAD_DIM, 1)
        mod_l = mod[l]
        res_ctx = None if l == 0 else (y_all, 0, mod[l - 1])
        res_lat = None if l == 0 else (y_all, tok_ctx, mod[l - 1])

        outs = _pre(xp, res_ctx, mod_l, 0, False, g1, wqkvT, wp, gq, gk, None, t_ctx, True)
        if l == 0:
            qT, k, vT, p, k32, v32 = outs
        else:
            qT, k, vT, p, xp, k32, v32 = outs
        aT = _attention(qT, k, vT, t_ctx, t_ctx, 1)
        xp, h2_all, route_ctx = _post(xp, aT, p, h2_all, 0, mod_l, 0, False, wo_top, wo_bot,
                                      pw, ps, g2, wr2, rb, t_ctx)
        new_k.append(k32.reshape(n_ctx, t_ctx, N_KV_HEADS, HEAD_DIM))
        new_v.append(v32.reshape(n_ctx, t_ctx, N_KV_HEADS, HEAD_DIM))

        outs = _pre(xs, res_lat, mod_l, 1, True, g1, wqkvT, wp, gq, gk, rope_tabs, tm_lat, False)
        if l == 0:
            qT, k, vT, p = outs
        else:
            qT, k, vT, p, xs = outs
        ck = cache_k[:, l].reshape(n_lat, past, KV_WIDTH).astype(QK8)
        cvT = cache_v[:, l].reshape(n_lat, past, KV_WIDTH).astype(QK8).transpose(0, 2, 1)
        k_all = jnp.concatenate([ck, k], axis=1)
        vT_all = jnp.concatenate([cvT, vT], axis=2)
        aT = _attention(qT, k_all, vT_all, 256, 512, 3)
        xs, h2_all, route_lat = _post(xs, aT, p, h2_all, tok_ctx, mod_l, 1, True, wo_top,
                                      wo_bot, pw, ps, g2, wr2, rb, 256)

        route_rows = jnp.concatenate(
            [route_ctx.transpose(0, 2, 1).reshape(tok_ctx, ROUTE_ROWS),
             route_lat.transpose(0, 2, 1).reshape(n_tokens - tok_ctx, ROUTE_ROWS)], axis=0)
        y_all = _experts(h2_all, y_all, _dispatch(route_rows, n_tokens, n_tiles), l, wg, wu, wd)

    mod_l = mod[DEPTH - 1]
    xp = _finish(xp, y_all, 0, mod_l, 0, False, t_ctx)
    xs = _finish(xs, y_all, tok_ctx, mod_l, 1, True, tm_lat)
    return xp, xs, jnp.stack(new_k, axis=1), jnp.stack(new_v, axis=1)
```
